```python
import jax, jax.numpy as jnp
from jax import lax
import numpy as np

D_MODEL = 1024
BATCH = 16
SEQ = 2048
DEPTH = 4

CHUNK = 64
Q_BLOCK = 128
MLA_HEADS = 8
QK_NOPE_DIM = 64
QK_ROPE_DIM = 32
QK_DIM = QK_NOPE_DIM + QK_ROPE_DIM
V_HEAD_DIM = 64
Q_RANK = 256
KV_RANK = 128
ROPE_THETA = 10000.0
MLA_WIDTH = MLA_HEADS * V_HEAD_DIM
GM_GROUPS = 8
GM_GROUP_DIM = 64
GM_WIDTH = GM_GROUPS * GM_GROUP_DIM
GM_BLOCK = 128
MIX_WIDTH = MLA_WIDTH + GM_WIDTH
IN_WIDTH = Q_RANK + KV_RANK + QK_ROPE_DIM + 2 * GM_WIDTH
N_EXPERT_GROUPS = 4
EXPERTS_PER_GROUP = 8
N_EXPERTS = N_EXPERT_GROUPS * EXPERTS_PER_GROUP
TOP_K_IN_GROUP = 2
D_EXPERT = 512
MOE_BLOCK = 128
PLE_DIM = 256
RMS_EPS = 1e-6

kernel_name = 'hybrid_mla_gmlp_hiermoe_ple'


def rms_norm(x, g):
    xf = x.astype(jnp.float32)
    y = xf * lax.rsqrt(jnp.mean(xf * xf, axis=-1, keepdims=True) + RMS_EPS)
    return (y * g.astype(jnp.float32)).astype(x.dtype)


def rope_tables(seq, dtype):
    inv = ROPE_THETA ** (-jnp.arange(0, QK_ROPE_DIM, 2, dtype=jnp.float32) / QK_ROPE_DIM)
    ang = jnp.arange(seq, dtype=jnp.float32)[:, None] * inv[None, :]
    return jnp.cos(ang).astype(dtype), jnp.sin(ang).astype(dtype)


def apply_rope(x, cos, sin):
    half = x.shape[-1] // 2
    x1, x2 = x[..., :half], x[..., half:]
    c = cos[None, :, None, :]
    s = sin[None, :, None, :]
    return jnp.concatenate([x1 * c - x2 * s, x2 * c + x1 * s], axis=-1)


def block_causal_attention(q, k, v):
    S = q.shape[1]
    scale = QK_DIM ** -0.5
    chunk_id = jnp.arange(S) // CHUNK
    neg = jnp.finfo(jnp.float32).min
    outs = []
    for blk in range(S // Q_BLOCK):
        q0 = blk * Q_BLOCK
        kend = q0 + Q_BLOCK
        qb = q[:, q0:kend]
        kb = k[:, :kend]
        vb = v[:, :kend]
        s = jnp.einsum('bqhd,bkhd->bhqk', qb, kb, preferred_element_type=jnp.float32) * scale
        mask = chunk_id[None, :kend] <= chunk_id[q0:kend, None]
        s = jnp.where(mask[None, None], s, neg)
        pr = jax.nn.softmax(s, axis=-1).astype(vb.dtype)
        outs.append(jnp.einsum('bhqk,bkhd->bqhd', pr, vb))
    return jnp.concatenate(outs, axis=1)


def mla_branch(c_q, c_kv, k_r, g_cq, g_ckv, w_uq, w_ukv, g_qn, g_kn, cos, sin):
    B, S, _ = c_q.shape
    q = (rms_norm(c_q, g_cq) @ w_uq).reshape(B, S, MLA_HEADS, QK_DIM)
    kv = (rms_norm(c_kv, g_ckv) @ w_ukv).reshape(B, S, MLA_HEADS, QK_NOPE_DIM + V_HEAD_DIM)
    k_nope, v = kv[..., :QK_NOPE_DIM], kv[..., QK_NOPE_DIM:]
    k_rope = jnp.broadcast_to(k_r[:, :, None, :], (B, S, MLA_HEADS, QK_ROPE_DIM))
    k = jnp.concatenate([k_nope, k_rope], axis=-1)
    q = rms_norm(q, g_qn)
    k = rms_norm(k, g_kn)
    q = jnp.concatenate([q[..., :QK_NOPE_DIM], apply_rope(q[..., QK_NOPE_DIM:], cos, sin)], axis=-1)
    k = jnp.concatenate([k[..., :QK_NOPE_DIM], apply_rope(k[..., QK_NOPE_DIM:], cos, sin)], axis=-1)
    o = block_causal_attention(q, k, v)
    return o.reshape(B, S, MLA_WIDTH)


def gmlp_branch(u_raw, v_raw, g_v, w_s, b_s):
    B, S, _ = u_raw.shape
    nb = S // GM_BLOCK
    u = jax.nn.gelu(u_raw).reshape(B, nb, GM_BLOCK, GM_GROUPS, GM_GROUP_DIM)
    v = rms_norm(jax.nn.gelu(v_raw).reshape(B, S, GM_GROUPS, GM_GROUP_DIM), g_v)
    v = v.reshape(B, nb, GM_BLOCK, GM_GROUPS, GM_GROUP_DIM)
    pos_chunk = jnp.arange(GM_BLOCK) // CHUNK
    mask = pos_chunk[None, :] <= pos_chunk[:, None]
    w = jnp.where(mask[None], w_s, jnp.zeros((), w_s.dtype))
    s = jnp.einsum('gtj,bnjgc->bntgc', w, v) + b_s.T[None, None, :, :, None]
    return (u * s).reshape(B, S, GM_WIDTH)


def hier_moe(xn, w_rg, w_re, w1, w3, w2):
    B, S, D = xn.shape
    T = B * S
    K = TOP_K_IN_GROUP
    xt = xn.reshape(T, D)
    lg = jnp.matmul(xt, w_rg, preferred_element_type=jnp.float32)
    pg = jax.nn.softmax(lg, axis=-1)
    g_sel = jnp.argmax(lg, axis=-1).astype(jnp.int32)
    p_sel = jnp.take_along_axis(pg, g_sel[:, None], axis=-1)
    le = jnp.matmul(xt, w_re, preferred_element_type=jnp.float32).reshape(T, N_EXPERT_GROUPS, EXPERTS_PER_GROUP)
    le_sel = jnp.take_along_axis(le, g_sel[:, None, None], axis=1)[:, 0]
    top_v, top_i = lax.top_k(le_sel, K)
    gates = p_sel * jax.nn.softmax(top_v, axis=-1)
    ids = (g_sel[:, None] * EXPERTS_PER_GROUP + top_i).reshape(-1).astype(jnp.int32)
    A = T * K
    tok = jnp.arange(A, dtype=jnp.int32) // K
    n_blocks = -(-A // MOE_BLOCK) + N_EXPERTS
    n_slots = n_blocks * MOE_BLOCK
    order = jnp.argsort(ids)
    s_ids = ids[order]
    counts = jnp.bincount(ids, length=N_EXPERTS).astype(jnp.int32)
    padded = (counts + MOE_BLOCK - 1) // MOE_BLOCK * MOE_BLOCK
    p_end = jnp.cumsum(padded)
    p_start = p_end - padded
    c_start = jnp.cumsum(counts) - counts
    dest = p_start[s_ids] + jnp.arange(A, dtype=jnp.int32) - c_start[s_ids]
    slot_tok = jnp.full((n_slots,), T, jnp.int32).at[dest].set(tok[order])
    slot_gate = jnp.zeros((n_slots,), jnp.float32).at[dest].set(gates.reshape(-1)[order])
    block_start = jnp.arange(n_blocks, dtype=jnp.int32) * MOE_BLOCK
    block_expert = jnp.minimum(jnp.sum(block_start[:, None] >= p_end[None, :], axis=1), N_EXPERTS - 1)
    x_pad = jnp.concatenate([xt, jnp.zeros((1, D), xt.dtype)], axis=0)
    x_slots = x_pad[slot_tok].reshape(n_blocks, MOE_BLOCK, D)

    def expert_block(args):
        xb, e = args
        return (jax.nn.silu(xb @ w1[e]) * (xb @ w3[e])) @ w2[e]

    y_slots = lax.map(expert_block, (x_slots, block_expert)).reshape(n_slots, D)
    y = jnp.zeros((T + 1, D), xt.dtype).at[slot_tok].add(y_slots * slot_gate[:, None].astype(xt.dtype))
    return y[:T].reshape(B, S, D)


def setup_inputs(seed: int = 0) -> dict:
    key = jax.random.key(seed)
    ks = jax.random.split(key, 32)
    L, D = DEPTH, D_MODEL

    def nrm(k, shape, scale):
        return jax.random.normal(k, shape, jnp.float32) * scale

    def gain(k, shape):
        return 1.0 + 0.1 * jax.random.normal(k, shape, jnp.float32)

    return {
        'x': nrm(ks[0], (BATCH, SEQ, D), 1.0),
        'p': nrm(ks[1], (DEPTH, BATCH, SEQ, PLE_DIM), 1.0),
        'g_mix_norm': gain(ks[2], (L, D)),
        'w_in': nrm(ks[3], (L, D, IN_WIDTH), D ** -0.5),
        'g_cq': gain(ks[4], (L, Q_RANK)),
        'g_ckv': gain(ks[5], (L, KV_RANK)),
        'w_uq': nrm(ks[6], (L, Q_RANK, MLA_HEADS * QK_DIM), Q_RANK ** -0.5),
        'w_ukv': nrm(ks[7], (L, KV_RANK, MLA_HEADS * (QK_NOPE_DIM + V_HEAD_DIM)), KV_RANK ** -0.5),
        'g_qn': gain(ks[8], (L, QK_DIM)),
        'g_kn': gain(ks[9], (L, QK_DIM)),
        'g_v': gain(ks[10], (L, GM_GROUPS, GM_GROUP_DIM)),
        'w_s': nrm(ks[11], (L, GM_GROUPS, GM_BLOCK, GM_BLOCK), GM_BLOCK ** -0.5),
        'b_s': gain(ks[12], (L, GM_GROUPS, GM_BLOCK)),
        'g_out_mla': gain(ks[13], (L, MLA_WIDTH)),
        'g_out_gmlp': gain(ks[14], (L, GM_WIDTH)),
        'w_out': nrm(ks[15], (L, MIX_WIDTH, D), MIX_WIDTH ** -0.5),
        'g_ffn_norm': gain(ks[16], (L, D)),
        'w_router_group': nrm(ks[17], (L, D, N_EXPERT_GROUPS), D ** -0.5),
        'w_router_expert': nrm(ks[18], (L, D, N_EXPERTS), D ** -0.5),
        'w1': nrm(ks[19], (L, N_EXPERTS, D, D_EXPERT), D ** -0.5),
        'w3': nrm(ks[20], (L, N_EXPERTS, D, D_EXPERT), D ** -0.5),
        'w2': nrm(ks[21], (L, N_EXPERTS, D_EXPERT, D), D_EXPERT ** -0.5),
        'g_ple': gain(ks[22], (L, D)),
        'w_ple_gate': nrm(ks[23], (L, D, D), D ** -0.5),
        'b_ple_gate': nrm(ks[24], (L, D), 0.02),
        'w_ple': nrm(ks[25], (L, PLE_DIM, D), PLE_DIM ** -0.5),
    }


def reference(x, p, g_mix_norm, w_in, g_cq, g_ckv, w_uq, w_ukv, g_qn, g_kn, g_v, w_s, b_s,
              g_out_mla, g_out_gmlp, w_out, g_ffn_norm, w_router_group, w_router_expert,
              w1, w3, w2, g_ple, w_ple_gate, b_ple_gate, w_ple):
    S = x.shape[1]
    cos, sin = rope_tables(S, x.dtype)
    s0 = Q_RANK
    s1 = s0 + KV_RANK
    s2 = s1 + QK_ROPE_DIM
    s3 = s2 + GM_WIDTH
    h = x
    for i in range(DEPTH):
        hn = rms_norm(h, g_mix_norm[i])
        proj = hn @ w_in[i]
        c_q = proj[..., :s0]
        c_kv = proj[..., s0:s1]
        k_r = proj[..., s1:s2]
        u_raw = proj[..., s2:s3]
        v_raw = proj[..., s3:]
        a = mla_branch(c_q, c_kv, k_r, g_cq[i], g_ckv[i], w_uq[i], w_ukv[i], g_qn[i], g_kn[i], cos, sin)
        m = gmlp_branch(u_raw, v_raw, g_v[i], w_s[i], b_s[i])
        mixed = jnp.concatenate([rms_norm(a, g_out_mla[i]), rms_norm(m, g_out_gmlp[i])], axis=-1)
        h = h + mixed @ w_out[i]
        h = h + hier_moe(rms_norm(h, g_ffn_norm[i]), w_router_group[i], w_router_expert[i], w1[i], w3[i], w2[i])
        gate = jax.nn.sigmoid(rms_norm(h, g_ple[i]) @ w_ple_gate[i] + b_ple_gate[i])
        h = h + (p[i] @ w_ple[i]) * gate
    return h
```

```python
import functools

import jax
import jax.numpy as jnp
from jax import lax
from jax.experimental import pallas as pl
from jax.experimental.pallas import tpu as pltpu

D_MODEL = 1024
CHUNK = 64
MLA_HEADS = 8
QK_NOPE_DIM = 64
QK_ROPE_DIM = 32
QK_DIM = QK_NOPE_DIM + QK_ROPE_DIM
V_HEAD_DIM = 64
Q_RANK = 256
KV_RANK = 128
ROPE_THETA = 10000.0
MLA_WIDTH = MLA_HEADS * V_HEAD_DIM
GM_GROUPS = 8
GM_GROUP_DIM = 64
GM_WIDTH = GM_GROUPS * GM_GROUP_DIM
GM_BLOCK = 128
N_EXPERT_GROUPS = 4
EXPERTS_PER_GROUP = 8
N_EXPERTS = N_EXPERT_GROUPS * EXPERTS_PER_GROUP
TOP_K = 2
D_EXPERT = 512
PLE_DIM = 256
RMS_EPS = 1e-6

LANES = 128
HEAD_PAD = LANES
QK_WIDTH = MLA_HEADS * HEAD_PAD
ROPE_HALF = QK_ROPE_DIM // 2
COL_CQ = 0
COL_CKV = COL_CQ + Q_RANK
COL_KR = COL_CKV + KV_RANK
COL_U = COL_KR + HEAD_PAD
COL_V = COL_U + GM_WIDTH
IN_COLS = COL_V + GM_WIDTH

TM_IN = 512
TQ = 256
TM_MIX = 256
BM = 256
TM_PLE = 512
VMEM_LIMIT = 48 * 1024 * 1024

F32 = jnp.float32
BF16 = jnp.bfloat16
NEG = float(jnp.finfo(jnp.float32).min)


def _dot(a, b):
    return jnp.dot(a, b, preferred_element_type=F32)


def _rms(x, g, width):
    ss = jnp.sum(x * x, axis=-1, keepdims=True) * (1.0 / width)
    return (x * lax.rsqrt(ss + RMS_EPS)) * g


def _inproj_kernel(h_ref, gmix_ref, win_ref, gcq_ref, gckv_ref, wuq_ref, wuk_ref, wuv_ref,
                   gq_ref, gk_ref, gv_ref, rc_ref, rs1_ref, rs2_ref,
                   q_ref, k_ref, v_ref, u_ref, vn_ref):
    h = h_ref[...]
    hn = _rms(h, gmix_ref[...], D_MODEL).astype(BF16)
    proj = _dot(hn, win_ref[...])
    cq = proj[:, COL_CQ:COL_CQ + Q_RANK]
    ckv = proj[:, COL_CKV:COL_CKV + KV_RANK]
    kr = proj[:, COL_KR:COL_KR + HEAD_PAD]
    cqn = _rms(cq, gcq_ref[...], Q_RANK).astype(BF16)
    ckvn = _rms(ckv, gckv_ref[...], KV_RANK).astype(BF16)
    q_raw = _dot(cqn, wuq_ref[...])
    k_nope = _dot(ckvn, wuk_ref[...])
    v_ref[...] = _dot(ckvn, wuv_ref[...]).astype(BF16)

    rc, rs1, rs2 = rc_ref[...], rs1_ref[...], rs2_ref[...]
    gq, gk = gq_ref[...], gk_ref[...]

    def norm_rope(x, g):
        x = _rms(x, g, QK_DIM)
        return x * rc + pltpu.roll(x, HEAD_PAD - ROPE_HALF, 1) * rs1 + pltpu.roll(x, ROPE_HALF, 1) * rs2

    for hd in range(MLA_HEADS):
        sl = slice(hd * HEAD_PAD, (hd + 1) * HEAD_PAD)
        q_ref[:, sl] = norm_rope(q_raw[:, sl], gq).astype(BF16)
        k_ref[:, sl] = norm_rope(k_nope[:, sl] + kr, gk).astype(BF16)

    u_ref[...] = jax.nn.gelu(proj[:, COL_U:COL_U + GM_WIDTH])
    gvv = jax.nn.gelu(proj[:, COL_V:COL_V + GM_WIDTH])
    lane = lax.broadcasted_iota(jnp.int32, (1, LANES), 1)
    low = lane < GM_GROUP_DIM
    gv = gv_ref[...]
    for j in range(GM_WIDTH // LANES):
        sl = slice(j * LANES, (j + 1) * LANES)
        x = gvv[:, sl]
        sq = x * x
        ss_lo = jnp.sum(jnp.where(low, sq, 0.0), axis=-1, keepdims=True)
        ss_hi = jnp.sum(jnp.where(low, 0.0, sq), axis=-1, keepdims=True)
        ms = jnp.where(low, ss_lo, ss_hi) * (1.0 / GM_GROUP_DIM)
        vn_ref[:, sl] = ((x * lax.rsqrt(ms + RMS_EPS)) * gv[:, sl]).astype(BF16)


def _inproj(h, lw, rope, seq):
    T = h.shape[0]
    nt = T // TM_IN
    per_seq = seq // TM_IN
    tok = lambda w: pl.BlockSpec((TM_IN, w), lambda i: (i, 0))
    full = lambda a: pl.BlockSpec(a.shape, lambda i: (0,) * a.ndim)
    pos = pl.BlockSpec((TM_IN, LANES), lambda i: (i % per_seq, 0))
    consts = [lw['g_mix'], lw['w_in'], lw['g_cq'], lw['g_ckv'], lw['w_uq'], lw['w_uk'], lw['w_uv'],
              lw['g_q'], lw['g_k'], lw['g_v']]
    return pl.pallas_call(
        _inproj_kernel,
        grid=(nt,),
        in_specs=[tok(D_MODEL)] + [full(a) for a in consts] + [pos, pos, pos],
        out_specs=[tok(QK_WIDTH), tok(QK_WIDTH), tok(MLA_WIDTH), tok(GM_WIDTH), tok(GM_WIDTH)],
        out_shape=[jax.ShapeDtypeStruct((T, QK_WIDTH), BF16), jax.ShapeDtypeStruct((T, QK_WIDTH), BF16),
                   jax.ShapeDtypeStruct((T, MLA_WIDTH), BF16), jax.ShapeDtypeStruct((T, GM_WIDTH), F32),
                   jax.ShapeDtypeStruct((T, GM_WIDTH), BF16)],
        compiler_params=pltpu.CompilerParams(dimension_semantics=("parallel",), vmem_limit_bytes=VMEM_LIMIT),
        name="inproj",
    )(h, *consts, *rope)


def _attn_kernel(q_ref, k_ref, v_ref, o_ref):
    qi = pl.program_id(2)
    row_chunk = lax.broadcasted_iota(jnp.int32, (TQ, TQ), 0) // CHUNK
    col_chunk = lax.broadcasted_iota(jnp.int32, (TQ, TQ), 1) // CHUNK
    diag_mask = col_chunk <= row_chunk
    outs = []
    for hh in range(2):
        q = q_ref[:, hh * HEAD_PAD:(hh + 1) * HEAD_PAD]

        def scores(kj):
            kb = k_ref[pl.ds(kj * TQ, TQ), hh * HEAD_PAD:(hh + 1) * HEAD_PAD]
            return lax.dot_general(q, kb, (((1,), (1,)), ((), ())), preferred_element_type=F32)

        def update(carry, s, kj):
            m, l, acc = carry
            m_new = jnp.maximum(m, jnp.max(s, axis=-1, keepdims=True))
            alpha = jnp.exp(m - m_new)
            p = jnp.exp(s - m_new)
            l = alpha * l + jnp.sum(p, axis=-1, keepdims=True)
            vb = v_ref[pl.ds(kj * TQ, TQ), hh * V_HEAD_DIM:(hh + 1) * V_HEAD_DIM]
            acc = alpha * acc + _dot(p.astype(BF16), vb)
            return m_new, l, acc

        def body(kj, carry):
            return update(carry, scores(kj), kj)

        init = (jnp.full((TQ, 1), NEG, F32), jnp.zeros((TQ, 1), F32), jnp.zeros((TQ, V_HEAD_DIM), F32))
        carry = lax.fori_loop(0, qi, body, init)
        s = jnp.where(diag_mask, scores(qi), NEG)
        m, l, acc = update(carry, s, qi)
        outs.append(acc / l)
    o_ref[...] = jnp.concatenate(outs, axis=-1)


def _attention(q, k, v, batch, seq):
    q3 = q.reshape(batch, seq, QK_WIDTH)
    k3 = k.reshape(batch, seq, QK_WIDTH)
    v3 = v.reshape(batch, seq, MLA_WIDTH)
    out = pl.pallas_call(
        _attn_kernel,
        grid=(batch, MLA_HEADS // 2, seq // TQ),
        in_specs=[pl.BlockSpec((None, TQ, 2 * HEAD_PAD), lambda b, hp, i: (b, i, hp)),
                  pl.BlockSpec((None, seq, 2 * HEAD_PAD), lambda b, hp, i: (b, 0, hp)),
                  pl.BlockSpec((None, seq, 2 * V_HEAD_DIM), lambda b, hp, i: (b, 0, hp))],
        out_specs=pl.BlockSpec((None, TQ, 2 * V_HEAD_DIM), lambda b, hp, i: (b, i, hp)),
        out_shape=jax.ShapeDtypeStruct((batch, seq, MLA_WIDTH), F32),
        compiler_params=pltpu.CompilerParams(dimension_semantics=("parallel", "parallel", "arbitrary"),
                                             vmem_limit_bytes=VMEM_LIMIT),
        name="attn",
    )(q3, k3, v3)
    return out.reshape(batch * seq, MLA_WIDTH)


def _mix_kernel(h_ref, a_ref, u_ref, vn_ref, ws_ref, bs_ref, goa_ref, gog_ref, wout_ref, gffn_ref, wr_ref,
                h1_ref, xn_ref, route_ref):
    t_chunk = lax.broadcasted_iota(jnp.int32, (GM_BLOCK, GM_BLOCK), 0) // CHUNK
    j_chunk = lax.broadcasted_iota(jnp.int32, (GM_BLOCK, GM_BLOCK), 1) // CHUNK
    w_mask = j_chunk <= t_chunk
    lane = lax.broadcasted_iota(jnp.int32, (1, LANES), 1)
    low = lane < GM_GROUP_DIM
    ws = [jnp.where(w_mask, ws_ref[g], 0.0).astype(BF16) for g in range(GM_GROUPS)]
    bs = bs_ref[...]
    m_blocks = []
    for nb in range(TM_MIX // GM_BLOCK):
        rows = slice(nb * GM_BLOCK, (nb + 1) * GM_BLOCK)
        cols = []
        for j in range(GM_WIDTH // LANES):
            sl = slice(j * LANES, (j + 1) * LANES)
            x = vn_ref[rows, sl]
            s = jnp.where(low, _dot(ws[2 * j], x), _dot(ws[2 * j + 1], x)) + bs[:, sl]
            cols.append(u_ref[rows, sl] * s)
        m_blocks.append(jnp.concatenate(cols, axis=-1))
    m = jnp.concatenate(m_blocks, axis=0)
    an = _rms(a_ref[...], goa_ref[...], MLA_WIDTH).astype(BF16)
    mn = _rms(m, gog_ref[...], GM_WIDTH).astype(BF16)
    wout = wout_ref[...]
    h1 = h_ref[...] + _dot(an, wout[:MLA_WIDTH]) + _dot(mn, wout[MLA_WIDTH:])
    h1_ref[...] = h1
    xn = _rms(h1, gffn_ref[...], D_MODEL).astype(BF16)
    xn_ref[...] = xn

    logits = _dot(xn, wr_ref[...])
    big = float(LANES)
    lane_f = lane.astype(F32)
    grp_f = (lane // EXPERTS_PER_GROUP).astype(F32)
    is_g = (lane >= N_EXPERTS) & (lane < N_EXPERTS + N_EXPERT_GROUPS)
    lg = jnp.where(is_g, logits, NEG)
    gmax = jnp.max(lg, axis=-1, keepdims=True)
    g_sel = jnp.min(jnp.where(is_g & (lg == gmax), lane_f - N_EXPERTS, big), axis=-1, keepdims=True)
    denom = jnp.sum(jnp.where(is_g, jnp.exp(lg - gmax), 0.0), axis=-1, keepdims=True)
    p_sel = 1.0 / denom
    in_grp = (lane < N_EXPERTS) & (grp_f == g_sel)
    le = jnp.where(in_grp, logits, NEG)
    t1 = jnp.max(le, axis=-1, keepdims=True)
    i1 = jnp.min(jnp.where(in_grp & (le == t1), lane_f, big), axis=-1, keepdims=True)
    in2 = in_grp & (lane_f != i1)
    le2 = jnp.where(in2, logits, NEG)
    t2 = jnp.max(le2, axis=-1, keepdims=True)
    i2 = jnp.min(jnp.where(in2 & (le2 == t2), lane_f, big), axis=-1, keepdims=True)
    e2 = jnp.exp(t2 - t1)
    tot = 1.0 + e2
    g1 = p_sel * (1.0 / tot)
    g2 = p_sel * (e2 / tot)
    route = jnp.where(lane == 0, i1,
                      jnp.where(lane == 1, i2,
                                jnp.where(lane == 2, g1, jnp.where(lane == 3, g2, 0.0))))
    route_ref[...] = route


def _mix(h, a, u, vn, lw):
    T = h.shape[0]
    tok = lambda w: pl.BlockSpec((TM_MIX, w), lambda i: (i, 0))
    full = lambda x: pl.BlockSpec(x.shape, lambda i: (0,) * x.ndim)
    consts = [lw['w_s'], lw['b_s'], lw['g_oa'], lw['g_og'], lw['w_out'], lw['g_ffn'], lw['w_r']]
    return pl.pallas_call(
        _mix_kernel,
        grid=(T // TM_MIX,),
        in_specs=[tok(D_MODEL), tok(MLA_WIDTH), tok(GM_WIDTH), tok(GM_WIDTH)] + [full(x) for x in consts],
        out_specs=[tok(D_MODEL), tok(D_MODEL), tok(LANES)],
        out_shape=[jax.ShapeDtypeStruct((T, D_MODEL), F32), jax.ShapeDtypeStruct((T, D_MODEL), BF16),
                   jax.ShapeDtypeStruct((T, LANES), F32)],
        compiler_params=pltpu.CompilerParams(dimension_semantics=("parallel",), vmem_limit_bytes=VMEM_LIMIT),
        name="mix",
    )(h, a, u, vn, *consts)


def _expert_kernel(be_ref, nused_ref, x_ref, w1_ref, w3_ref, w2_ref, y_ref):
    i = pl.program_id(0)

    @pl.when(i < nused_ref[0])
    def _():
        x = x_ref[...]
        a = _dot(x, w1_ref[...])
        b = _dot(x, w3_ref[...])
        act = (jax.nn.silu(a) * b).astype(BF16)
        y_ref[...] = _dot(act, w2_ref[...])

    @pl.when(i >= nused_ref[0])
    def _():
        y_ref[...] = jnp.zeros_like(y_ref)


def _experts(x_slots, block_expert, n_used, lw):
    n_slots = x_slots.shape[0]
    grid_spec = pltpu.PrefetchScalarGridSpec(
        num_scalar_prefetch=2,
        grid=(n_slots // BM,),
        in_specs=[pl.BlockSpec((BM, D_MODEL), lambda i, be, nu: (i, 0)),
                  pl.BlockSpec((None, D_MODEL, D_EXPERT), lambda i, be, nu: (be[i], 0, 0)),
                  pl.BlockSpec((None, D_MODEL, D_EXPERT), lambda i, be, nu: (be[i], 0, 0)),
                  pl.BlockSpec((None, D_EXPERT, D_MODEL), lambda i, be, nu: (be[i], 0, 0))],
        out_specs=pl.BlockSpec((BM, D_MODEL), lambda i, be, nu: (i, 0)),
    )
    return pl.pallas_call(
        _expert_kernel,
        grid_spec=grid_spec,
        out_shape=jax.ShapeDtypeStruct((n_slots, D_MODEL), F32),
        compiler_params=pltpu.CompilerParams(dimension_semantics=("arbitrary",), vmem_limit_bytes=VMEM_LIMIT),
        name="experts",
    )(block_expert, n_used, x_slots, lw['w1'], lw['w3'], lw['w2'])


def _ple_kernel(h_ref, y_ref, route_ref, p_ref, gple_ref, wg_ref, bg_ref, wp_ref, o_ref):
    route = route_ref[...]
    g1 = route[:, 2:3]
    g2 = route[:, 3:4]
    h2 = h_ref[...] + (y_ref[:, :D_MODEL] * g1 + y_ref[:, D_MODEL:] * g2)
    hn = _rms(h2, gple_ref[...], D_MODEL).astype(BF16)
    gate = jax.nn.sigmoid(_dot(hn, wg_ref[...]) + bg_ref[...])
    pw = _dot(p_ref[...].astype(BF16), wp_ref[...])
    o_ref[...] = h2 + pw * gate


def _ple(h1, ysel, route, p, lw):
    T = h1.shape[0]
    tok = lambda w: pl.BlockSpec((TM_PLE, w), lambda i: (i, 0))
    full = lambda x: pl.BlockSpec(x.shape, lambda i: (0,) * x.ndim)
    consts = [lw['g_ple'], lw['w_gate'], lw['b_gate'], lw['w_ple']]
    return pl.pallas_call(
        _ple_kernel,
        grid=(T // TM_PLE,),
        in_specs=[tok(D_MODEL), tok(TOP_K * D_MODEL), tok(LANES), tok(PLE_DIM)] + [full(x) for x in consts],
        out_specs=tok(D_MODEL),
        out_shape=jax.ShapeDtypeStruct((T, D_MODEL), F32),
        compiler_params=pltpu.CompilerParams(dimension_semantics=("parallel",), vmem_limit_bytes=VMEM_LIMIT),
        name="ple",
    )(h1, ysel, route, p, *consts)


def _pad_heads(w, used):
    lead = w.shape[:-1]
    w = w.reshape(lead + (MLA_HEADS, used))
    w = jnp.pad(w, [(0, 0)] * len(lead) + [(0, 0), (0, HEAD_PAD - used)])
    return w.reshape(lead + (MLA_HEADS * HEAD_PAD,))


def _pad_head_vec(g):
    return jnp.pad(g, (0, HEAD_PAD - g.shape[0])).reshape(1, HEAD_PAD)


def _layer_weights(i, w):
    s0 = Q_RANK
    s1 = s0 + KV_RANK
    s2 = s1 + QK_ROPE_DIM
    s3 = s2 + GM_WIDTH
    w_in = w['w_in'][i]
    kr_cols = jnp.pad(w_in[:, s1:s2], ((0, 0), (QK_NOPE_DIM, HEAD_PAD - QK_DIM)))
    w_in_l = jnp.concatenate([w_in[:, :s1], kr_cols, w_in[:, s2:s3], w_in[:, s3:]], axis=1)
    w_ukv = w['w_ukv'][i].reshape(KV_RANK, MLA_HEADS, QK_NOPE_DIM + V_HEAD_DIM)
    w_uk = _pad_heads(w_ukv[:, :, :QK_NOPE_DIM].reshape(KV_RANK, MLA_HEADS * QK_NOPE_DIM), QK_NOPE_DIM)
    w_uv = w_ukv[:, :, QK_NOPE_DIM:].reshape(KV_RANK, MLA_WIDTH)
    w_r = jnp.concatenate([w['w_router_expert'][i], w['w_router_group'][i]], axis=1)
    w_r = jnp.pad(w_r, ((0, 0), (0, LANES - w_r.shape[1])))
    b_s = jnp.repeat(w['b_s'][i].T, GM_GROUP_DIM, axis=1)
    return {
        'g_mix': w['g_mix_norm'][i].reshape(1, -1),
        'w_in': w_in_l.astype(BF16),
        'g_cq': w['g_cq'][i].reshape(1, -1),
        'g_ckv': w['g_ckv'][i].reshape(1, -1),
        'w_uq': _pad_heads(w['w_uq'][i], QK_DIM).astype(BF16),
        'w_uk': w_uk.astype(BF16),
        'w_uv': w_uv.astype(BF16),
        'g_q': _pad_head_vec(w['g_qn'][i] * (QK_DIM ** -0.5)),
        'g_k': _pad_head_vec(w['g_kn'][i]),
        'g_v': w['g_v'][i].reshape(1, GM_WIDTH),
        'w_s': w['w_s'][i],
        'b_s': b_s,
        'g_oa': w['g_out_mla'][i].reshape(1, -1),
        'g_og': w['g_out_gmlp'][i].reshape(1, -1),
        'w_out': w['w_out'][i].astype(BF16),
        'g_ffn': w['g_ffn_norm'][i].reshape(1, -1),
        'w_r': w_r.astype(BF16),
        'w1': w['w1'][i].astype(BF16),
        'w3': w['w3'][i].astype(BF16),
        'w2': w['w2'][i].astype(BF16),
        'g_ple': w['g_ple'][i].reshape(1, -1),
        'w_gate': w['w_ple_gate'][i].astype(BF16),
        'b_gate': w['b_ple_gate'][i].reshape(1, -1),
        'w_ple': w['w_ple'][i].astype(BF16),
    }


def _rope_tables(seq):
    inv = ROPE_THETA ** (-jnp.arange(0, QK_ROPE_DIM, 2, dtype=F32) / QK_ROPE_DIM)
    ang = jnp.arange(seq, dtype=F32)[:, None] * inv[None, :]
    cos, sin = jnp.cos(ang), jnp.sin(ang)
    z = lambda n: jnp.zeros((seq, n), F32)
    rc = jnp.concatenate([jnp.ones((seq, QK_NOPE_DIM), F32), cos, cos, z(HEAD_PAD - QK_DIM)], axis=1)
    rs1 = jnp.concatenate([z(QK_NOPE_DIM), -sin, z(HEAD_PAD - QK_NOPE_DIM - ROPE_HALF)], axis=1)
    rs2 = jnp.concatenate([z(QK_NOPE_DIM + ROPE_HALF), sin, z(HEAD_PAD - QK_DIM)], axis=1)
    return rc, rs1, rs2


def _dispatch_plan(ids):
    A = ids.size
    flat = ids.reshape(A)
    onehot = (flat[:, None] == jnp.arange(N_EXPERTS, dtype=jnp.int32)[None, :]).astype(jnp.int32)
    csum = jnp.cumsum(onehot, axis=0)
    counts = csum[-1]
    rank = jnp.sum(onehot * csum, axis=1) - 1
    padded = (counts + BM - 1) // BM * BM
    p_end = jnp.cumsum(padded)
    p_start = p_end - padded
    dest = p_start[flat] + rank
    n_blocks = A // BM + N_EXPERTS
    block_start = jnp.arange(n_blocks, dtype=jnp.int32) * BM
    block_expert = jnp.minimum(jnp.sum(block_start[:, None] >= p_end[None, :], axis=1), N_EXPERTS - 1)
    n_used = (p_end[-1] // BM).astype(jnp.int32).reshape(1)
    return dest.astype(jnp.int32), block_expert.astype(jnp.int32), n_used, n_blocks * BM


def kernel(x, p, g_mix_norm, w_in, g_cq, g_ckv, w_uq, w_ukv, g_qn, g_kn, g_v, w_s, b_s, g_out_mla, g_out_gmlp,
           w_out, g_ffn_norm, w_router_group, w_router_expert, w1, w3, w2, g_ple, w_ple_gate, b_ple_gate, w_ple):
    w = dict(g_mix_norm=g_mix_norm, w_in=w_in, g_cq=g_cq, g_ckv=g_ckv, w_uq=w_uq, w_ukv=w_ukv, g_qn=g_qn,
             g_kn=g_kn, g_v=g_v, w_s=w_s, b_s=b_s, g_out_mla=g_out_mla, g_out_gmlp=g_out_gmlp, w_out=w_out,
             g_ffn_norm=g_ffn_norm, w_router_group=w_router_group, w_router_expert=w_router_expert,
             w1=w1, w3=w3, w2=w2, g_ple=g_ple, w_ple_gate=w_ple_gate, b_ple_gate=b_ple_gate, w_ple=w_ple)
    batch, seq, d = x.shape
    depth = p.shape[0]
    T = batch * seq
    rope = _rope_tables(seq)
    h = x.reshape(T, d)
    for i in range(depth):
        lw = _layer_weights(i, w)
        q, k, v, u, vn = _inproj(h, lw, rope, seq)
        a = _attention(q, k, v, batch, seq)
        h1, xn, route = _mix(h, a, u, vn, lw)
        ids = route[:, :TOP_K].astype(jnp.int32)
        dest, block_expert, n_used, n_slots = _dispatch_plan(ids)
        tok = jnp.arange(T * TOP_K, dtype=jnp.int32) // TOP_K
        slot_tok = jnp.zeros((n_slots,), jnp.int32).at[dest].set(tok)
        x_slots = xn[slot_tok]
        y_slots = _experts(x_slots, block_expert, n_used, lw)
        ysel = y_slots[dest].reshape(T, TOP_K * d)
        h = _ple(h1, ysel, route, p[i].reshape(T, PLE_DIM), lw)
    return h.reshape(batch, seq, d)
```

```python
import functools

import jax
import jax.numpy as jnp
from jax import lax
from jax.experimental import pallas as pl
from jax.experimental.pallas import tpu as pltpu

D_MODEL = 1024
CHUNK = 64
MLA_HEADS = 8
QK_NOPE_DIM = 64
QK_ROPE_DIM = 32
QK_DIM = QK_NOPE_DIM + QK_ROPE_DIM
V_HEAD_DIM = 64
Q_RANK = 256
KV_RANK = 128
ROPE_THETA = 10000.0
MLA_WIDTH = MLA_HEADS * V_HEAD_DIM
GM_GROUPS = 8
GM_GROUP_DIM = 64
GM_WIDTH = GM_GROUPS * GM_GROUP_DIM
GM_BLOCK = 128
N_EXPERT_GROUPS = 4
EXPERTS_PER_GROUP = 8
N_EXPERTS = N_EXPERT_GROUPS * EXPERTS_PER_GROUP
TOP_K = 2
D_EXPERT = 512
PLE_DIM = 256
RMS_EPS = 1e-6

LANES = 128
HEAD_PAD = LANES
QK_WIDTH = MLA_HEADS * HEAD_PAD
ROPE_HALF = QK_ROPE_DIM // 2
COL_CQ = 0
COL_CKV = COL_CQ + Q_RANK
COL_KR = COL_CKV + KV_RANK
COL_U = COL_KR + HEAD_PAD
COL_V = COL_U + GM_WIDTH
IN_COLS = COL_V + GM_WIDTH

TM_IN = 512
TQ = 256
TM_MIX = 256
BM = 256
TM_PLE = 512
VMEM_LIMIT = 48 * 1024 * 1024

F32 = jnp.float32
BF16 = jnp.bfloat16
NEG = float(jnp.finfo(jnp.float32).min)


def _dot(a, b):
    return jnp.dot(a, b, preferred_element_type=F32)


def _rms(x, g, width):
    ss = jnp.sum(x * x, axis=-1, keepdims=True) * (1.0 / width)
    return (x * lax.rsqrt(ss + RMS_EPS)) * g


def _inproj_kernel(h_ref, gmix_ref, win_ref, gcq_ref, gckv_ref, wuq_ref, wuk_ref, wuv_ref,
                   gq_ref, gk_ref, gv_ref, rc_ref, rs1_ref, rs2_ref,
                   q_ref, k_ref, v_ref, u_ref, vn_ref):
    h = h_ref[...]
    hn = _rms(h, gmix_ref[...], D_MODEL).astype(BF16)
    proj = _dot(hn, win_ref[...])
    cq = proj[:, COL_CQ:COL_CQ + Q_RANK]
    ckv = proj[:, COL_CKV:COL_CKV + KV_RANK]
    kr = proj[:, COL_KR:COL_KR + HEAD_PAD]
    cqn = _rms(cq, gcq_ref[...], Q_RANK).astype(BF16)
    ckvn = _rms(ckv, gckv_ref[...], KV_RANK).astype(BF16)
    q_raw = _dot(cqn, wuq_ref[...])
    k_nope = _dot(ckvn, wuk_ref[...])
    v_ref[...] = _dot(ckvn, wuv_ref[...]).astype(BF16)

    rc, rs1, rs2 = rc_ref[...], rs1_ref[...], rs2_ref[...]
    gq, gk = gq_ref[...], gk_ref[...]

    def norm_rope(x, g):
        x = _rms(x, g, QK_DIM)
        return x * rc + pltpu.roll(x, HEAD_PAD - ROPE_HALF, 1) * rs1 + pltpu.roll(x, ROPE_HALF, 1) * rs2

    for hd in range(MLA_HEADS):
        sl = slice(hd * HEAD_PAD, (hd + 1) * HEAD_PAD)
        q_ref[:, sl] = norm_rope(q_raw[:, sl], gq).astype(BF16)
        k_ref[:, sl] = norm_rope(k_nope[:, sl] + kr, gk).astype(BF16)

    u_ref[...] = jax.nn.gelu(proj[:, COL_U:COL_U + GM_WIDTH])
    gvv = jax.nn.gelu(proj[:, COL_V:COL_V + GM_WIDTH])
    lane = lax.broadcasted_iota(jnp.int32, (1, LANES), 1)
    low = lane < GM_GROUP_DIM
    gv = gv_ref[...]
    for j in range(GM_WIDTH // LANES):
        sl = slice(j * LANES, (j + 1) * LANES)
        x = gvv[:, sl]
        sq = x * x
        ss_lo = jnp.sum(jnp.where(low, sq, 0.0), axis=-1, keepdims=True)
        ss_hi = jnp.sum(jnp.where(low, 0.0, sq), axis=-1, keepdims=True)
        ms = jnp.where(low, ss_lo, ss_hi) * (1.0 / GM_GROUP_DIM)
        vn_ref[:, sl] = ((x * lax.rsqrt(ms + RMS_EPS)) * gv[:, sl]).astype(BF16)


def _inproj(h, lw, rope, seq):
    T = h.shape[0]
    nt = T // TM_IN
    per_seq = seq // TM_IN
    tok = lambda w: pl.BlockSpec((TM_IN, w), lambda i: (i, 0))
    full = lambda a: pl.BlockSpec(a.shape, lambda i: (0,) * a.ndim)
    pos = pl.BlockSpec((TM_IN, LANES), lambda i: (i % per_seq, 0))
    consts = [lw['g_mix'], lw['w_in'], lw['g_cq'], lw['g_ckv'], lw['w_uq'], lw['w_uk'], lw['w_uv'],
              lw['g_q'], lw['g_k'], lw['g_v']]
    return pl.pallas_call(
        _inproj_kernel,
        grid=(nt,),
        in_specs=[tok(D_MODEL)] + [full(a) for a in consts] + [pos, pos, pos],
        out_specs=[tok(QK_WIDTH), tok(QK_WIDTH), tok(MLA_WIDTH), tok(GM_WIDTH), tok(GM_WIDTH)],
        out_shape=[jax.ShapeDtypeStruct((T, QK_WIDTH), BF16), jax.ShapeDtypeStruct((T, QK_WIDTH), BF16),
                   jax.ShapeDtypeStruct((T, MLA_WIDTH), BF16), jax.ShapeDtypeStruct((T, GM_WIDTH), F32),
                   jax.ShapeDtypeStruct((T, GM_WIDTH), BF16)],
        compiler_params=pltpu.CompilerParams(dimension_semantics=("parallel",), vmem_limit_bytes=VMEM_LIMIT),
        name="inproj",
    )(h, *consts, *rope)


def _attn_tile(q_ref, k_ref, v_ref, o_ref, t):
    l0 = t * TQ
    row_chunk = lax.broadcasted_iota(jnp.int32, (TQ, TQ), 0) // CHUNK
    col_chunk = lax.broadcasted_iota(jnp.int32, (TQ, TQ), 1) // CHUNK
    diag_mask = col_chunk <= row_chunk
    low = lax.broadcasted_iota(jnp.int32, (1, LANES), 1) < V_HEAD_DIM
    nt = (((1,), (1,)), ((), ()))

    def head_pair(hp, carry):
        vcol = pl.multiple_of(hp * LANES, LANES)
        v_diag = v_ref[l0:l0 + TQ, pl.ds(vcol, LANES)]
        outs = []
        for hh in range(2):
            col = pl.multiple_of(hp * (2 * HEAD_PAD) + hh * HEAD_PAD, HEAD_PAD)
            q = q_ref[:, pl.ds(col, HEAD_PAD)]
            s_d = lax.dot_general(q, k_ref[l0:l0 + TQ, pl.ds(col, HEAD_PAD)], nt, preferred_element_type=F32)
            s_d = jnp.where(diag_mask, s_d, NEG)
            m = jnp.max(s_d, axis=-1, keepdims=True)
            if t > 0:
                s_m = lax.dot_general(q, k_ref[0:l0, pl.ds(col, HEAD_PAD)], nt, preferred_element_type=F32)
                m = jnp.maximum(m, jnp.max(s_m, axis=-1, keepdims=True))
                p_m = jnp.exp(s_m - m)
                l = jnp.sum(p_m, axis=-1, keepdims=True)
                o = _dot(p_m.astype(BF16), v_ref[0:l0, pl.ds(vcol, LANES)])
            p_d = jnp.exp(s_d - m)
            l_d = jnp.sum(p_d, axis=-1, keepdims=True)
            o_d = _dot(p_d.astype(BF16), v_diag)
            if t > 0:
                l, o = l + l_d, o + o_d
            else:
                l, o = l_d, o_d
            outs.append(o * (1.0 / l))
        o_ref[:, pl.ds(vcol, LANES)] = jnp.where(low, outs[0], outs[1])
        return carry

    lax.fori_loop(0, MLA_HEADS // 2, head_pair, 0)


def _attn_kernel(q_ref, k_ref, v_ref, o_ref):
    qi = pl.program_id(1)
    for t in range(k_ref.shape[0] // TQ):
        pl.when(qi == t)(functools.partial(_attn_tile, q_ref, k_ref, v_ref, o_ref, t))


def _attention(q, k, v, batch, seq):
    q3 = q.reshape(batch, seq, QK_WIDTH)
    k3 = k.reshape(batch, seq, QK_WIDTH)
    v3 = v.reshape(batch, seq, MLA_WIDTH)
    out = pl.pallas_call(
        _attn_kernel,
        grid=(batch, seq // TQ),
        in_specs=[pl.BlockSpec((None, TQ, QK_WIDTH), lambda b, i: (b, i, 0)),
                  pl.BlockSpec((None, seq, QK_WIDTH), lambda b, i: (b, 0, 0)),
                  pl.BlockSpec((None, seq, MLA_WIDTH), lambda b, i: (b, 0, 0))],
        out_specs=pl.BlockSpec((None, TQ, MLA_WIDTH), lambda b, i: (b, i, 0)),
        out_shape=jax.ShapeDtypeStruct((batch, seq, MLA_WIDTH), F32),
        compiler_params=pltpu.CompilerParams(dimension_semantics=("parallel", "arbitrary"),
                                             vmem_limit_bytes=VMEM_LIMIT),
        name="attn",
    )(q3, k3, v3)
    return out.reshape(batch * seq, MLA_WIDTH)


def _mix_kernel(h_ref, a_ref, u_ref, vn_ref, ws_ref, bs_ref, goa_ref, gog_ref, wout_ref, gffn_ref, wr_ref,
                h1_ref, xn_ref, route_ref):
    t_chunk = lax.broadcasted_iota(jnp.int32, (GM_BLOCK, GM_BLOCK), 0) // CHUNK
    j_chunk = lax.broadcasted_iota(jnp.int32, (GM_BLOCK, GM_BLOCK), 1) // CHUNK
    w_mask = j_chunk <= t_chunk
    lane = lax.broadcasted_iota(jnp.int32, (1, LANES), 1)
    low = lane < GM_GROUP_DIM
    ws = [jnp.where(w_mask, ws_ref[g], 0.0).astype(BF16) for g in range(GM_GROUPS)]
    bs = bs_ref[...]
    m_blocks = []
    for nb in range(TM_MIX // GM_BLOCK):
        rows = slice(nb * GM_BLOCK, (nb + 1) * GM_BLOCK)
        cols = []
        for j in range(GM_WIDTH // LANES):
            sl = slice(j * LANES, (j + 1) * LANES)
            x = vn_ref[rows, sl]
            s = jnp.where(low, _dot(ws[2 * j], x), _dot(ws[2 * j + 1], x)) + bs[:, sl]
            cols.append(u_ref[rows, sl] * s)
        m_blocks.append(jnp.concatenate(cols, axis=-1))
    m = jnp.concatenate(m_blocks, axis=0)
    an = _rms(a_ref[...], goa_ref[...], MLA_WIDTH).astype(BF16)
    mn = _rms(m, gog_ref[...], GM_WIDTH).astype(BF16)
    wout = wout_ref[...]
    h1 = h_ref[...] + _dot(an, wout[:MLA_WIDTH]) + _dot(mn, wout[MLA_WIDTH:])
    h1_ref[...] = h1
    xn = _rms(h1, gffn_ref[...], D_MODEL).astype(BF16)
    xn_ref[...] = xn

    logits = _dot(xn, wr_ref[...])
    big = float(LANES)
    lane_f = lane.astype(F32)
    grp_f = (lane // EXPERTS_PER_GROUP).astype(F32)
    is_g = (lane >= N_EXPERTS) & (lane < N_EXPERTS + N_EXPERT_GROUPS)
    lg = jnp.where(is_g, logits, NEG)
    gmax = jnp.max(lg, axis=-1, keepdims=True)
    g_sel = jnp.min(jnp.where(is_g & (lg == gmax), lane_f - N_EXPERTS, big), axis=-1, keepdims=True)
    denom = jnp.sum(jnp.where(is_g, jnp.exp(lg - gmax), 0.0), axis=-1, keepdims=True)
    p_sel = 1.0 / denom
    in_grp = (lane < N_EXPERTS) & (grp_f == g_sel)
    le = jnp.where(in_grp, logits, NEG)
    t1 = jnp.max(le, axis=-1, keepdims=True)
    i1 = jnp.min(jnp.where(in_grp & (le == t1), lane_f, big), axis=-1, keepdims=True)
    in2 = in_grp & (lane_f != i1)
    le2 = jnp.where(in2, logits, NEG)
    t2 = jnp.max(le2, axis=-1, keepdims=True)
    i2 = jnp.min(jnp.where(in2 & (le2 == t2), lane_f, big), axis=-1, keepdims=True)
    e2 = jnp.exp(t2 - t1)
    tot = 1.0 + e2
    g1 = p_sel * (1.0 / tot)
    g2 = p_sel * (e2 / tot)
    route = jnp.where(lane == 0, i1,
                      jnp.where(lane == 1, i2,
                                jnp.where(lane == 2, g1, jnp.where(lane == 3, g2, 0.0))))
    route_ref[...] = route


def _mix(h, a, u, vn, lw):
    T = h.shape[0]
    tok = lambda w: pl.BlockSpec((TM_MIX, w), lambda i: (i, 0))
    full = lambda x: pl.BlockSpec(x.shape, lambda i: (0,) * x.ndim)
    consts = [lw['w_s'], lw['b_s'], lw['g_oa'], lw['g_og'], lw['w_out'], lw['g_ffn'], lw['w_r']]
    return pl.pallas_call(
        _mix_kernel,
        grid=(T // TM_MIX,),
        in_specs=[tok(D_MODEL), tok(MLA_WIDTH), tok(GM_WIDTH), tok(GM_WIDTH)] + [full(x) for x in consts],
        out_specs=[tok(D_MODEL), tok(D_MODEL), tok(LANES)],
        out_shape=[jax.ShapeDtypeStruct((T, D_MODEL), F32), jax.ShapeDtypeStruct((T, D_MODEL), BF16),
                   jax.ShapeDtypeStruct((T, LANES), F32)],
        compiler_params=pltpu.CompilerParams(dimension_semantics=("parallel",), vmem_limit_bytes=VMEM_LIMIT),
        name="mix",
    )(h, a, u, vn, *consts)


def _expert_kernel(be_ref, nused_ref, x_ref, w1_ref, w3_ref, w2_ref, y_ref):
    i = pl.program_id(0)

    @pl.when(i < nused_ref[0])
    def _():
        x = x_ref[...]
        a = _dot(x, w1_ref[...])
        b = _dot(x, w3_ref[...])
        act = (jax.nn.silu(a) * b).astype(BF16)
        y_ref[...] = _dot(act, w2_ref[...])

    @pl.when(i >= nused_ref[0])
    def _():
        y_ref[...] = jnp.zeros_like(y_ref)


def _experts(x_slots, block_expert, n_used, lw):
    n_slots = x_slots.shape[0]
    grid_spec = pltpu.PrefetchScalarGridSpec(
        num_scalar_prefetch=2,
        grid=(n_slots // BM,),
        in_specs=[pl.BlockSpec((BM, D_MODEL), lambda i, be, nu: (i, 0)),
                  pl.BlockSpec((None, D_MODEL, D_EXPERT), lambda i, be, nu: (be[i], 0, 0)),
                  pl.BlockSpec((None, D_MODEL, D_EXPERT), lambda i, be, nu: (be[i], 0, 0)),
                  pl.BlockSpec((None, D_EXPERT, D_MODEL), lambda i, be, nu: (be[i], 0, 0))],
        out_specs=pl.BlockSpec((BM, D_MODEL), lambda i, be, nu: (i, 0)),
    )
    return pl.pallas_call(
        _expert_kernel,
        grid_spec=grid_spec,
        out_shape=jax.ShapeDtypeStruct((n_slots, D_MODEL), F32),
        compiler_params=pltpu.CompilerParams(dimension_semantics=("arbitrary",), vmem_limit_bytes=VMEM_LIMIT),
        name="experts",
    )(block_expert, n_used, x_slots, lw['w1'], lw['w3'], lw['w2'])


def _ple_kernel(h_ref, y_ref, route_ref, p_ref, gple_ref, wg_ref, bg_ref, wp_ref, o_ref):
    route = route_ref[...]
    g1 = route[:, 2:3]
    g2 = route[:, 3:4]
    h2 = h_ref[...] + (y_ref[:, :D_MODEL] * g1 + y_ref[:, D_MODEL:] * g2)
    hn = _rms(h2, gple_ref[...], D_MODEL).astype(BF16)
    gate = jax.nn.sigmoid(_dot(hn, wg_ref[...]) + bg_ref[...])
    pw = _dot(p_ref[...].astype(BF16), wp_ref[...])
    o_ref[...] = h2 + pw * gate


def _ple(h1, ysel, route, p, lw):
    T = h1.shape[0]
    tok = lambda w: pl.BlockSpec((TM_PLE, w), lambda i: (i, 0))
    full = lambda x: pl.BlockSpec(x.shape, lambda i: (0,) * x.ndim)
    consts = [lw['g_ple'], lw['w_gate'], lw['b_gate'], lw['w_ple']]
    return pl.pallas_call(
        _ple_kernel,
        grid=(T // TM_PLE,),
        in_specs=[tok(D_MODEL), tok(TOP_K * D_MODEL), tok(LANES), tok(PLE_DIM)] + [full(x) for x in consts],
        out_specs=tok(D_MODEL),
        out_shape=jax.ShapeDtypeStruct((T, D_MODEL), F32),
        compiler_params=pltpu.CompilerParams(dimension_semantics=("parallel",), vmem_limit_bytes=VMEM_LIMIT),
        name="ple",
    )(h1, ysel, route, p, *consts)


def _pad_heads(w, used):
    lead = w.shape[:-1]
    w = w.reshape(lead + (MLA_HEADS, used))
    w = jnp.pad(w, [(0, 0)] * len(lead) + [(0, 0), (0, HEAD_PAD - used)])
    return w.reshape(lead + (MLA_HEADS * HEAD_PAD,))


def _pad_head_vec(g):
    return jnp.pad(g, (0, HEAD_PAD - g.shape[0])).reshape(1, HEAD_PAD)


def _layer_weights(i, w):
    s0 = Q_RANK
    s1 = s0 + KV_RANK
    s2 = s1 + QK_ROPE_DIM
    s3 = s2 + GM_WIDTH
    w_in = w['w_in'][i]
    kr_cols = jnp.pad(w_in[:, s1:s2], ((0, 0), (QK_NOPE_DIM, HEAD_PAD - QK_DIM)))
    w_in_l = jnp.concatenate([w_in[:, :s1], kr_cols, w_in[:, s2:s3], w_in[:, s3:]], axis=1)
    w_ukv = w['w_ukv'][i].reshape(KV_RANK, MLA_HEADS, QK_NOPE_DIM + V_HEAD_DIM)
    w_uk = _pad_heads(w_ukv[:, :, :QK_NOPE_DIM].reshape(KV_RANK, MLA_HEADS * QK_NOPE_DIM), QK_NOPE_DIM)
    w_uv = w_ukv[:, :, QK_NOPE_DIM:].reshape(KV_RANK, MLA_WIDTH)
    w_r = jnp.concatenate([w['w_router_expert'][i], w['w_router_group'][i]], axis=1)
    w_r = jnp.pad(w_r, ((0, 0), (0, LANES - w_r.shape[1])))
    b_s = jnp.repeat(w['b_s'][i].T, GM_GROUP_DIM, axis=1)
    return {
        'g_mix': w['g_mix_norm'][i].reshape(1, -1),
        'w_in': w_in_l.astype(BF16),
        'g_cq': w['g_cq'][i].reshape(1, -1),
        'g_ckv': w['g_ckv'][i].reshape(1, -1),
        'w_uq': _pad_heads(w['w_uq'][i], QK_DIM).astype(BF16),
        'w_uk': w_uk.astype(BF16),
        'w_uv': w_uv.astype(BF16),
        'g_q': _pad_head_vec(w['g_qn'][i] * (QK_DIM ** -0.5)),
        'g_k': _pad_head_vec(w['g_kn'][i]),
        'g_v': w['g_v'][i].reshape(1, GM_WIDTH),
        'w_s': w['w_s'][i],
        'b_s': b_s,
        'g_oa': w['g_out_mla'][i].reshape(1, -1),
        'g_og': w['g_out_gmlp'][i].reshape(1, -1),
        'w_out': w['w_out'][i].astype(BF16),
        'g_ffn': w['g_ffn_norm'][i].reshape(1, -1),
        'w_r': w_r.astype(BF16),
        'w1': w['w1'][i].astype(BF16),
        'w3': w['w3'][i].astype(BF16),
        'w2': w['w2'][i].astype(BF16),
        'g_ple': w['g_ple'][i].reshape(1, -1),
        'w_gate': w['w_ple_gate'][i].astype(BF16),
        'b_gate': w['b_ple_gate'][i].reshape(1, -1),
        'w_ple': w['w_ple'][i].astype(BF16),
    }


def _rope_tables(seq):
    inv = ROPE_THETA ** (-jnp.arange(0, QK_ROPE_DIM, 2, dtype=F32) / QK_ROPE_DIM)
    ang = jnp.arange(seq, dtype=F32)[:, None] * inv[None, :]
    cos, sin = jnp.cos(ang), jnp.sin(ang)
    z = lambda n: jnp.zeros((seq, n), F32)
    rc = jnp.concatenate([jnp.ones((seq, QK_NOPE_DIM), F32), cos, cos, z(HEAD_PAD - QK_DIM)], axis=1)
    rs1 = jnp.concatenate([z(QK_NOPE_DIM), -sin, z(HEAD_PAD - QK_NOPE_DIM - ROPE_HALF)], axis=1)
    rs2 = jnp.concatenate([z(QK_NOPE_DIM + ROPE_HALF), sin, z(HEAD_PAD - QK_DIM)], axis=1)
    return rc, rs1, rs2


def _dispatch_plan(ids):
    A = ids.size
    flat = ids.reshape(A)
    onehot = (flat[:, None] == jnp.arange(N_EXPERTS, dtype=jnp.int32)[None, :]).astype(jnp.int32)
    csum = jnp.cumsum(onehot, axis=0)
    counts = csum[-1]
    rank = jnp.sum(onehot * csum, axis=1) - 1
    padded = (counts + BM - 1) // BM * BM
    p_end = jnp.cumsum(padded)
    p_start = p_end - padded
    dest = p_start[flat] + rank
    n_blocks = A // BM + N_EXPERTS
    block_start = jnp.arange(n_blocks, dtype=jnp.int32) * BM
    block_expert = jnp.minimum(jnp.sum(block_start[:, None] >= p_end[None, :], axis=1), N_EXPERTS - 1)
    n_used = (p_end[-1] // BM).astype(jnp.int32).reshape(1)
    return dest.astype(jnp.int32), block_expert.astype(jnp.int32), n_used, n_blocks * BM


def kernel(x, p, g_mix_norm, w_in, g_cq, g_ckv, w_uq, w_ukv, g_qn, g_kn, g_v, w_s, b_s, g_out_mla, g_out_gmlp,
           w_out, g_ffn_norm, w_router_group, w_router_expert, w1, w3, w2, g_ple, w_ple_gate, b_ple_gate, w_ple):
    w = dict(g_mix_norm=g_mix_norm, w_in=w_in, g_cq=g_cq, g_ckv=g_ckv, w_uq=w_uq, w_ukv=w_ukv, g_qn=g_qn,
             g_kn=g_kn, g_v=g_v, w_s=w_s, b_s=b_s, g_out_mla=g_out_mla, g_out_gmlp=g_out_gmlp, w_out=w_out,
             g_ffn_norm=g_ffn_norm, w_router_group=w_router_group, w_router_expert=w_router_expert,
             w1=w1, w3=w3, w2=w2, g_ple=g_ple, w_ple_gate=w_ple_gate, b_ple_gate=b_ple_gate, w_ple=w_ple)
    batch, seq, d = x.shape
    depth = p.shape[0]
    T = batch * seq
    rope = _rope_tables(seq)
    h = x.reshape(T, d)
    for i in range(depth):
        lw = _layer_weights(i, w)
        q, k, v, u, vn = _inproj(h, lw, rope, seq)
        a = _attention(q, k, v, batch, seq)
        h1, xn, route = _mix(h, a, u, vn, lw)
        ids = route[:, :TOP_K].astype(jnp.int32)
        dest, block_expert, n_used, n_slots = _dispatch_plan(ids)
        tok = jnp.arange(T * TOP_K, dtype=jnp.int32) // TOP_K
        slot_tok = jnp.zeros((n_slots,), jnp.int32).at[dest].set(tok)
        x_slots = xn[slot_tok]
        y_slots = _experts(x_slots, block_expert, n_used, lw)
        ysel = y_slots[dest].reshape(T, TOP_K * d)
        h = _ple(h1, ysel, route, p[i].reshape(T, PLE_DIM), lw)
    return h.reshape(batch, seq, d)
```

```python
import functools

import jax
import jax.numpy as jnp
from jax import lax
from jax.experimental import pallas as pl
from jax.experimental.pallas import tpu as pltpu

D_MODEL = 1024
CHUNK = 64
MLA_HEADS = 8
QK_NOPE_DIM = 64
QK_ROPE_DIM = 32
QK_DIM = QK_NOPE_DIM + QK_ROPE_DIM
V_HEAD_DIM = 64
Q_RANK = 256
KV_RANK = 128
ROPE_THETA = 10000.0
MLA_WIDTH = MLA_HEADS * V_HEAD_DIM
GM_GROUPS = 8
GM_GROUP_DIM = 64
GM_WIDTH = GM_GROUPS * GM_GROUP_DIM
GM_BLOCK = 128
N_EXPERT_GROUPS = 4
EXPERTS_PER_GROUP = 8
N_EXPERTS = N_EXPERT_GROUPS * EXPERTS_PER_GROUP
TOP_K = 2
D_EXPERT = 512
PLE_DIM = 256
RMS_EPS = 1e-6

LANES = 128
HEAD_PAD = LANES
QK_WIDTH = MLA_HEADS * HEAD_PAD
ROPE_HALF = QK_ROPE_DIM // 2
COL_CQ = 0
COL_CKV = COL_CQ + Q_RANK
COL_KR = COL_CKV + KV_RANK
COL_U = COL_KR + HEAD_PAD
COL_V = COL_U + GM_WIDTH
IN_COLS = COL_V + GM_WIDTH

SUBLANES = 8

TM_IN = 512
TQ = 256
TM_MIX = 256
BM = 256
TM_MOE = 512
SORTED_ROWS = -(-(TOP_K * TM_MOE + N_EXPERTS * (SUBLANES - 1)) // LANES) * LANES
N_CHUNKS = SORTED_ROWS // SUBLANES
MAX_PAD_CHUNKS = BM // SUBLANES
VMEM_LIMIT = 48 * 1024 * 1024

F32 = jnp.float32
BF16 = jnp.bfloat16
NEG = float(jnp.finfo(jnp.float32).min)


def _dot(a, b):
    return jnp.dot(a, b, preferred_element_type=F32)


def _rms(x, g, width):
    ss = jnp.sum(x * x, axis=-1, keepdims=True) * (1.0 / width)
    return (x * lax.rsqrt(ss + RMS_EPS)) * g


def _inproj_kernel(h_ref, gmix_ref, win_ref, gcq_ref, gckv_ref, wuq_ref, wuk_ref, wuv_ref,
                   gq_ref, gk_ref, gv_ref, rc_ref, rs_ref,
                   q_ref, k_ref, v_ref, u_ref, vn_ref):
    h = h_ref[...]
    hn = _rms(h, gmix_ref[...], D_MODEL).astype(BF16)
    proj = _dot(hn, win_ref[...])
    cq = proj[:, COL_CQ:COL_CQ + Q_RANK]
    ckv = proj[:, COL_CKV:COL_CKV + KV_RANK]
    kr = proj[:, COL_KR:COL_KR + HEAD_PAD]
    cqn = _rms(cq, gcq_ref[...], Q_RANK).astype(BF16)
    ckvn = _rms(ckv, gckv_ref[...], KV_RANK).astype(BF16)
    q_raw = _dot(cqn, wuq_ref[...])
    k_nope = _dot(ckvn, wuk_ref[...])
    v_ref[...] = _dot(ckvn, wuv_ref[...]).astype(BF16)

    rc, rs = rc_ref[...], rs_ref[...]
    gq, gk = gq_ref[...], gk_ref[...]

    def rope(x):
        return x * rc + pltpu.roll(x, HEAD_PAD // 2, 1) * rs

    rk = rope(kr * gk)
    ss_kr = jnp.sum(kr * kr, axis=-1, keepdims=True)
    for hd in range(MLA_HEADS):
        sl = slice(hd * HEAD_PAD, (hd + 1) * HEAD_PAD)
        q_ref[:, sl] = rope(_rms(q_raw[:, sl], gq, QK_DIM)).astype(BF16)
        kn = k_nope[:, sl]
        ms = (jnp.sum(kn * kn, axis=-1, keepdims=True) + ss_kr) * (1.0 / QK_DIM)
        k_ref[:, sl] = ((kn * gk + rk) * lax.rsqrt(ms + RMS_EPS)).astype(BF16)

    u_ref[...] = jax.nn.gelu(proj[:, COL_U:COL_U + GM_WIDTH])
    gvv = jax.nn.gelu(proj[:, COL_V:COL_V + GM_WIDTH])
    lane = lax.broadcasted_iota(jnp.int32, (1, LANES), 1)
    low = lane < GM_GROUP_DIM
    gv = gv_ref[...]
    for j in range(GM_WIDTH // LANES):
        sl = slice(j * LANES, (j + 1) * LANES)
        x = gvv[:, sl]
        sq = x * x
        ss_lo = jnp.sum(jnp.where(low, sq, 0.0), axis=-1, keepdims=True)
        ss_hi = jnp.sum(jnp.where(low, 0.0, sq), axis=-1, keepdims=True)
        ms = jnp.where(low, ss_lo, ss_hi) * (1.0 / GM_GROUP_DIM)
        vn_ref[:, sl] = ((x * lax.rsqrt(ms + RMS_EPS)) * gv[:, sl]).astype(BF16)


def _inproj(h, lw, rope, seq):
    T = h.shape[0]
    nt = T // TM_IN
    per_seq = seq // TM_IN
    tok = lambda w: pl.BlockSpec((TM_IN, w), lambda i: (i, 0))
    full = lambda a: pl.BlockSpec(a.shape, lambda i: (0,) * a.ndim)
    pos = pl.BlockSpec((TM_IN, LANES), lambda i: (i % per_seq, 0))
    consts = [lw['g_mix'], lw['w_in'], lw['g_cq'], lw['g_ckv'], lw['w_uq'], lw['w_uk'], lw['w_uv'],
              lw['g_q'], lw['g_k'], lw['g_v']]
    return pl.pallas_call(
        _inproj_kernel,
        grid=(nt,),
        in_specs=[tok(D_MODEL)] + [full(a) for a in consts] + [pos, pos],
        out_specs=[tok(QK_WIDTH), tok(QK_WIDTH), tok(MLA_WIDTH), tok(GM_WIDTH), tok(GM_WIDTH)],
        out_shape=[jax.ShapeDtypeStruct((T, QK_WIDTH), BF16), jax.ShapeDtypeStruct((T, QK_WIDTH), BF16),
                   jax.ShapeDtypeStruct((T, MLA_WIDTH), BF16), jax.ShapeDtypeStruct((T, GM_WIDTH), F32),
                   jax.ShapeDtypeStruct((T, GM_WIDTH), BF16)],
        compiler_params=pltpu.CompilerParams(dimension_semantics=("parallel",), vmem_limit_bytes=VMEM_LIMIT),
        name="inproj",
    )(h, *consts, *rope)


def _attn_tile(q_ref, k_ref, v_ref, o_ref, t):
    l0 = t * TQ
    row_chunk = lax.broadcasted_iota(jnp.int32, (TQ, TQ), 0) // CHUNK
    col_chunk = lax.broadcasted_iota(jnp.int32, (TQ, TQ), 1) // CHUNK
    diag_mask = col_chunk <= row_chunk
    low = lax.broadcasted_iota(jnp.int32, (1, LANES), 1) < V_HEAD_DIM
    nt = (((1,), (1,)), ((), ()))

    def head_pair(hp, carry):
        vcol = pl.multiple_of(hp * LANES, LANES)
        v_diag = v_ref[l0:l0 + TQ, pl.ds(vcol, LANES)]
        outs = []
        for hh in range(2):
            col = pl.multiple_of(hp * (2 * HEAD_PAD) + hh * HEAD_PAD, HEAD_PAD)
            q = q_ref[:, pl.ds(col, HEAD_PAD)]
            s_d = lax.dot_general(q, k_ref[l0:l0 + TQ, pl.ds(col, HEAD_PAD)], nt, preferred_element_type=F32)
            s_d = jnp.where(diag_mask, s_d, NEG)
            m = jnp.max(s_d, axis=-1, keepdims=True)
            if t > 0:
                s_m = lax.dot_general(q, k_ref[0:l0, pl.ds(col, HEAD_PAD)], nt, preferred_element_type=F32)
                m = jnp.maximum(m, jnp.max(s_m, axis=-1, keepdims=True))
                p_m = jnp.exp(s_m - m)
                l = jnp.sum(p_m, axis=-1, keepdims=True)
                o = _dot(p_m.astype(BF16), v_ref[0:l0, pl.ds(vcol, LANES)])
            p_d = jnp.exp(s_d - m)
            l_d = jnp.sum(p_d, axis=-1, keepdims=True)
            o_d = _dot(p_d.astype(BF16), v_diag)
            if t > 0:
                l, o = l + l_d, o + o_d
            else:
                l, o = l_d, o_d
            outs.append(o * (1.0 / l))
        o_ref[:, pl.ds(vcol, LANES)] = jnp.where(low, outs[0], outs[1])
        return carry

    lax.fori_loop(0, MLA_HEADS // 2, head_pair, 0)


def _attn_kernel(q_ref, k_ref, v_ref, o_ref):
    qi = pl.program_id(1)
    for t in range(k_ref.shape[0] // TQ):
        pl.when(qi == t)(functools.partial(_attn_tile, q_ref, k_ref, v_ref, o_ref, t))


def _attention(q, k, v, batch, seq):
    q3 = q.reshape(batch, seq, QK_WIDTH)
    k3 = k.reshape(batch, seq, QK_WIDTH)
    v3 = v.reshape(batch, seq, MLA_WIDTH)
    out = pl.pallas_call(
        _attn_kernel,
        grid=(batch, seq // TQ),
        in_specs=[pl.BlockSpec((None, TQ, QK_WIDTH), lambda b, i: (b, i, 0)),
                  pl.BlockSpec((None, seq, QK_WIDTH), lambda b, i: (b, 0, 0)),
                  pl.BlockSpec((None, seq, MLA_WIDTH), lambda b, i: (b, 0, 0))],
        out_specs=pl.BlockSpec((None, TQ, MLA_WIDTH), lambda b, i: (b, i, 0)),
        out_shape=jax.ShapeDtypeStruct((batch, seq, MLA_WIDTH), F32),
        compiler_params=pltpu.CompilerParams(dimension_semantics=("parallel", "arbitrary"),
                                             vmem_limit_bytes=VMEM_LIMIT),
        name="attn",
    )(q3, k3, v3)
    return out.reshape(batch * seq, MLA_WIDTH)


def _mix_kernel(h_ref, a_ref, u_ref, vn_ref, ws_ref, bs_ref, goa_ref, gog_ref, wout_ref, gffn_ref, wr_ref,
                h1_ref, xn_ref, route_ref, cnt_ref):
    t_chunk = lax.broadcasted_iota(jnp.int32, (GM_BLOCK, GM_BLOCK), 0) // CHUNK
    j_chunk = lax.broadcasted_iota(jnp.int32, (GM_BLOCK, GM_BLOCK), 1) // CHUNK
    w_mask = j_chunk <= t_chunk
    lane = lax.broadcasted_iota(jnp.int32, (1, LANES), 1)
    low = lane < GM_GROUP_DIM
    ws = [jnp.where(w_mask, ws_ref[g], 0.0).astype(BF16) for g in range(GM_GROUPS)]
    bs = bs_ref[...]
    m_blocks = []
    for nb in range(TM_MIX // GM_BLOCK):
        rows = slice(nb * GM_BLOCK, (nb + 1) * GM_BLOCK)
        cols = []
        for j in range(GM_WIDTH // LANES):
            sl = slice(j * LANES, (j + 1) * LANES)
            x = vn_ref[rows, sl]
            s = jnp.where(low, _dot(ws[2 * j], x), _dot(ws[2 * j + 1], x)) + bs[:, sl]
            cols.append(u_ref[rows, sl] * s)
        m_blocks.append(jnp.concatenate(cols, axis=-1))
    m = jnp.concatenate(m_blocks, axis=0)
    an = _rms(a_ref[...], goa_ref[...], MLA_WIDTH).astype(BF16)
    mn = _rms(m, gog_ref[...], GM_WIDTH).astype(BF16)
    wout = wout_ref[...]
    h1 = h_ref[...] + _dot(an, wout[:MLA_WIDTH]) + _dot(mn, wout[MLA_WIDTH:])
    h1_ref[...] = h1
    xn = _rms(h1, gffn_ref[...], D_MODEL).astype(BF16)
    xn_ref[...] = xn

    logits = _dot(xn, wr_ref[...])
    big = float(LANES)
    lane_f = lane.astype(F32)
    grp_f = (lane // EXPERTS_PER_GROUP).astype(F32)
    is_g = (lane >= N_EXPERTS) & (lane < N_EXPERTS + N_EXPERT_GROUPS)
    lg = jnp.where(is_g, logits, NEG)
    gmax = jnp.max(lg, axis=-1, keepdims=True)
    g_sel = jnp.min(jnp.where(is_g & (lg == gmax), lane_f - N_EXPERTS, big), axis=-1, keepdims=True)
    denom = jnp.sum(jnp.where(is_g, jnp.exp(lg - gmax), 0.0), axis=-1, keepdims=True)
    p_sel = 1.0 / denom
    in_grp = (lane < N_EXPERTS) & (grp_f == g_sel)
    le = jnp.where(in_grp, logits, NEG)
    t1 = jnp.max(le, axis=-1, keepdims=True)
    i1 = jnp.min(jnp.where(in_grp & (le == t1), lane_f, big), axis=-1, keepdims=True)
    in2 = in_grp & (lane_f != i1)
    le2 = jnp.where(in2, logits, NEG)
    t2 = jnp.max(le2, axis=-1, keepdims=True)
    i2 = jnp.min(jnp.where(in2 & (le2 == t2), lane_f, big), axis=-1, keepdims=True)
    e2 = jnp.exp(t2 - t1)
    tot = 1.0 + e2
    g1 = p_sel * (1.0 / tot)
    g2 = p_sel * (e2 / tot)
    route = jnp.where(lane == 0, i1,
                      jnp.where(lane == 1, i2,
                                jnp.where(lane == 2, g1, jnp.where(lane == 3, g2, 0.0))))
    route_ref[...] = route
    hits = jnp.where((lane_f == i1) | (lane_f == i2), 1.0, 0.0)
    cnt_ref[...] = jnp.broadcast_to(jnp.sum(hits, axis=0, keepdims=True), (SUBLANES, LANES))


def _mix(h, a, u, vn, lw):
    T = h.shape[0]
    nt = T // TM_MIX
    tok = lambda w: pl.BlockSpec((TM_MIX, w), lambda i: (i, 0))
    full = lambda x: pl.BlockSpec(x.shape, lambda i: (0,) * x.ndim)
    consts = [lw['w_s'], lw['b_s'], lw['g_oa'], lw['g_og'], lw['w_out'], lw['g_ffn'], lw['w_r']]
    return pl.pallas_call(
        _mix_kernel,
        grid=(nt,),
        in_specs=[tok(D_MODEL), tok(MLA_WIDTH), tok(GM_WIDTH), tok(GM_WIDTH)] + [full(x) for x in consts],
        out_specs=[tok(D_MODEL), tok(D_MODEL), tok(LANES), pl.BlockSpec((None, SUBLANES, LANES), lambda i: (i, 0, 0))],
        out_shape=[jax.ShapeDtypeStruct((T, D_MODEL), F32), jax.ShapeDtypeStruct((T, D_MODEL), BF16),
                   jax.ShapeDtypeStruct((T, LANES), F32), jax.ShapeDtypeStruct((nt, SUBLANES, LANES), F32)],
        compiler_params=pltpu.CompilerParams(dimension_semantics=("parallel",), vmem_limit_bytes=VMEM_LIMIT),
        name="mix",
    )(h, a, u, vn, *consts)


def _chunk_copy(src_ref, src_row, dst_ref, dst_row, sem):
    return pltpu.make_async_copy(src_ref.at[pl.ds(pl.multiple_of(src_row, SUBLANES), SUBLANES), :],
                                 dst_ref.at[pl.ds(pl.multiple_of(dst_row, SUBLANES), SUBLANES), :], sem)


def _dispatch_kernel(cdst_ref, nch_ref, zstart_ref, nz_ref, nused_ref, route_ref, xn_ref, lstart_ref,
                     xs_hbm, pos_ref, sorted_buf, zero_buf, sems, zsem):
    i = pl.program_id(0)
    nt = pl.num_programs(0)
    slot = i % 2
    route = route_ref[...]
    lane = lax.broadcasted_iota(jnp.int32, (1, LANES), 1)
    lane_f = lane.astype(F32)
    oh0 = lane_f == route[:, 0:1]
    oh1 = lane_f == route[:, 1:2]
    earlier = lax.broadcasted_iota(jnp.int32, (TM_MOE, TM_MOE), 1) < lax.broadcasted_iota(jnp.int32, (TM_MOE, TM_MOE), 0)
    tri = jnp.where(earlier, 1.0, 0.0).astype(BF16)
    oh0_f = jnp.where(oh0, 1.0, 0.0)
    c0 = _dot(tri, oh0_f.astype(BF16))
    c1 = _dot(tri, jnp.where(oh1, 1.0, 0.0).astype(BF16))
    tot0 = jnp.sum(oh0_f, axis=0, keepdims=True)
    ls = lstart_ref[0:1, :]
    pos0 = jnp.sum(jnp.where(oh0, ls + c0, 0.0), axis=-1, keepdims=True)
    pos1 = jnp.sum(jnp.where(oh1, ls + tot0 + c1, 0.0), axis=-1, keepdims=True)
    pos = jnp.where(lane == 0, pos0, jnp.where(lane == 1, pos1, 0.0))
    pos_ref[...] = pos
    pos_t = pos.T
    r0 = pos_t[0:1, :].astype(jnp.int32)
    r1 = pos_t[1:2, :].astype(jnp.int32)
    rid = lax.broadcasted_iota(jnp.int32, (SORTED_ROWS, TM_MOE), 0)
    perm = jnp.where((rid == r0) | (rid == r1), 1.0, 0.0).astype(BF16)
    sorted_buf[slot] = _dot(perm, xn_ref[...])

    def issue(c, carry):
        _chunk_copy(sorted_buf.at[slot], c * SUBLANES, xs_hbm, cdst_ref[i * N_CHUNKS + c], sems.at[slot]).start()
        return carry

    lax.fori_loop(0, nch_ref[i], issue, 0)

    @pl.when(i == 0)
    def _():
        zero_buf[...] = jnp.zeros_like(zero_buf)

        def per_expert(e, carry):
            def zissue(j, c2):
                _chunk_copy(zero_buf, 0, xs_hbm, zstart_ref[e] + j * SUBLANES, zsem).start()
                return c2

            def zwait(j, c2):
                _chunk_copy(zero_buf, 0, xs_hbm, 0, zsem).wait()
                return c2

            lax.fori_loop(0, nz_ref[e], zissue, 0)
            lax.fori_loop(0, nz_ref[e], zwait, 0)
            return carry

        lax.fori_loop(0, N_EXPERTS, per_expert, 0)

        def tail_copy(b):
            return pltpu.make_async_copy(zero_buf, xs_hbm.at[pl.ds(pl.multiple_of(b * BM, BM), BM), :], zsem)

        def tail_issue(b, carry):
            tail_copy(b).start()
            return carry

        def tail_wait(b, carry):
            tail_copy(b).wait()
            return carry

        n_blocks = xs_hbm.shape[0] // BM
        lax.fori_loop(nused_ref[0], n_blocks, tail_issue, 0)
        lax.fori_loop(nused_ref[0], n_blocks, tail_wait, 0)

    def drain(n, sl):
        def wait(c, carry):
            _chunk_copy(sorted_buf.at[sl], 0, xs_hbm, 0, sems.at[sl]).wait()
            return carry

        lax.fori_loop(0, n, wait, 0)

    @pl.when(i > 0)
    def _():
        drain(nch_ref[i - 1], 1 - slot)

    @pl.when(i == nt - 1)
    def _():
        drain(nch_ref[i], slot)


def _dispatch(route, xn, plan, n_slots):
    T = route.shape[0]
    nt = T // TM_MOE
    grid_spec = pltpu.PrefetchScalarGridSpec(
        num_scalar_prefetch=5,
        grid=(nt,),
        in_specs=[pl.BlockSpec((TM_MOE, LANES), lambda i, *_: (i, 0)),
                  pl.BlockSpec((TM_MOE, D_MODEL), lambda i, *_: (i, 0)),
                  pl.BlockSpec((None, SUBLANES, LANES), lambda i, *_: (i, 0, 0))],
        out_specs=[pl.BlockSpec(memory_space=pl.ANY),
                   pl.BlockSpec((TM_MOE, LANES), lambda i, *_: (i, 0))],
        scratch_shapes=[pltpu.VMEM((2, SORTED_ROWS, D_MODEL), F32), pltpu.VMEM((BM, D_MODEL), F32),
                        pltpu.SemaphoreType.DMA((2,)), pltpu.SemaphoreType.DMA(())],
    )
    return pl.pallas_call(
        _dispatch_kernel,
        grid_spec=grid_spec,
        out_shape=[jax.ShapeDtypeStruct((n_slots, D_MODEL), F32), jax.ShapeDtypeStruct((T, LANES), F32)],
        compiler_params=pltpu.CompilerParams(dimension_semantics=("arbitrary",), vmem_limit_bytes=VMEM_LIMIT),
        name="dispatch",
    )(plan['cdst'], plan['nch'], plan['zstart'], plan['nz'], plan['n_used'], route, xn, plan['lstart'])


def _expert_kernel(be_ref, nused_ref, x_ref, w1_ref, w3_ref, w2_ref, y_ref, w1_b, w3_b, w2_b):
    i = pl.program_id(0)

    @pl.when((i == 0) | (be_ref[i] != be_ref[jnp.maximum(i - 1, 0)]))
    def _():
        w1_b[...] = w1_ref[...].astype(BF16)
        w3_b[...] = w3_ref[...].astype(BF16)
        w2_b[...] = w2_ref[...].astype(BF16)

    @pl.when(i < nused_ref[0])
    def _():
        x = x_ref[...].astype(BF16)
        a = _dot(x, w1_b[...])
        b = _dot(x, w3_b[...])
        act = (jax.nn.silu(a) * b).astype(BF16)
        y_ref[...] = _dot(act, w2_b[...])

    @pl.when(i >= nused_ref[0])
    def _():
        y_ref[...] = jnp.zeros_like(y_ref)


def _experts(x_slots, plan, w1, w3, w2, layer):
    n_slots = x_slots.shape[0]
    grid_spec = pltpu.PrefetchScalarGridSpec(
        num_scalar_prefetch=2,
        grid=(n_slots // BM,),
        in_specs=[pl.BlockSpec((BM, D_MODEL), lambda i, be, nu: (jnp.minimum(i, nu[0] - 1), 0)),
                  pl.BlockSpec((None, None, D_MODEL, D_EXPERT), lambda i, be, nu: (layer, be[i], 0, 0)),
                  pl.BlockSpec((None, None, D_MODEL, D_EXPERT), lambda i, be, nu: (layer, be[i], 0, 0)),
                  pl.BlockSpec((None, None, D_EXPERT, D_MODEL), lambda i, be, nu: (layer, be[i], 0, 0))],
        out_specs=pl.BlockSpec((BM, D_MODEL), lambda i, be, nu: (i, 0)),
        scratch_shapes=[pltpu.VMEM((D_MODEL, D_EXPERT), BF16), pltpu.VMEM((D_MODEL, D_EXPERT), BF16),
                        pltpu.VMEM((D_EXPERT, D_MODEL), BF16)],
    )
    return pl.pallas_call(
        _expert_kernel,
        grid_spec=grid_spec,
        out_shape=jax.ShapeDtypeStruct((n_slots, D_MODEL), F32),
        compiler_params=pltpu.CompilerParams(dimension_semantics=("arbitrary",), vmem_limit_bytes=VMEM_LIMIT),
        name="experts",
    )(plan['block_expert'], plan['n_used'], x_slots, w1, w3, w2)


def _ple_kernel(cdst_ref, nch_ref, h_ref, pos_ref, route_ref, p_ref, gple_ref, wg_ref, bg_ref, wp_ref, y_hbm,
                o_ref, y_buf, sems):
    i = pl.program_id(0)
    nt = pl.num_programs(0)
    slot = i % 2

    def gather(tile, sl):
        def issue(c, carry):
            _chunk_copy(y_hbm, cdst_ref[tile * N_CHUNKS + c], y_buf.at[sl], c * SUBLANES, sems.at[sl]).start()
            return carry

        lax.fori_loop(0, nch_ref[tile], issue, 0)

    @pl.when(i == 0)
    def _():
        y_buf[...] = jnp.zeros_like(y_buf)
        gather(0, 0)

    @pl.when(i + 1 < nt)
    def _():
        gather(i + 1, 1 - slot)

    def wait(c, carry):
        _chunk_copy(y_hbm, 0, y_buf.at[slot], 0, sems.at[slot]).wait()
        return carry

    lax.fori_loop(0, nch_ref[i], wait, 0)

    route = route_ref[...]
    pos = pos_ref[...]
    row = lax.broadcasted_iota(jnp.int32, (1, SORTED_ROWS), 1).astype(F32)
    unsort = (jnp.where(row == pos[:, 0:1], route[:, 2:3], 0.0)
              + jnp.where(row == pos[:, 1:2], route[:, 3:4], 0.0)).astype(BF16)
    h2 = h_ref[...] + _dot(unsort, y_buf[slot].astype(BF16))
    hn = _rms(h2, gple_ref[...], D_MODEL).astype(BF16)
    gate = jax.nn.sigmoid(_dot(hn, wg_ref[...]) + bg_ref[...])
    pw = _dot(p_ref[...].astype(BF16), wp_ref[...])
    o_ref[...] = h2 + pw * gate


def _ple(h1, y_slots, pos, route, p, plan, lw):
    T = h1.shape[0]
    tok = lambda w: pl.BlockSpec((TM_MOE, w), lambda i, *_: (i, 0))
    full = lambda x: pl.BlockSpec(x.shape, lambda i, *_: (0,) * x.ndim)
    consts = [lw['g_ple'], lw['w_gate'], lw['b_gate'], lw['w_ple']]
    grid_spec = pltpu.PrefetchScalarGridSpec(
        num_scalar_prefetch=2,
        grid=(T // TM_MOE,),
        in_specs=[tok(D_MODEL), tok(LANES), tok(LANES), tok(PLE_DIM)] + [full(x) for x in consts]
                 + [pl.BlockSpec(memory_space=pl.ANY)],
        out_specs=tok(D_MODEL),
        scratch_shapes=[pltpu.VMEM((2, SORTED_ROWS, D_MODEL), F32), pltpu.SemaphoreType.DMA((2,))],
    )
    return pl.pallas_call(
        _ple_kernel,
        grid_spec=grid_spec,
        out_shape=jax.ShapeDtypeStruct((T, D_MODEL), F32),
        compiler_params=pltpu.CompilerParams(dimension_semantics=("arbitrary",), vmem_limit_bytes=VMEM_LIMIT),
        name="ple",
    )(plan['cdst'], plan['nch'], h1, pos, route, p, *consts, y_slots)


def _head_lane_src():
    src = [QK_DIM] * HEAD_PAD
    for j in range(ROPE_HALF):
        src[j] = QK_NOPE_DIM + j
        src[HEAD_PAD // 2 + j] = QK_NOPE_DIM + ROPE_HALF + j
    nope_a = HEAD_PAD // 2 - ROPE_HALF
    for j in range(nope_a):
        src[ROPE_HALF + j] = j
    for j in range(QK_NOPE_DIM - nope_a):
        src[HEAD_PAD // 2 + ROPE_HALF + j] = nope_a + j
    return jnp.asarray(src, dtype=jnp.int32)


def _lay_heads(w, heads):
    lead = w.shape[:-1]
    w = w.reshape(lead + (heads, QK_DIM))
    w = jnp.pad(w, [(0, 0)] * (len(lead) + 1) + [(0, 1)])
    w = jnp.take(w, _head_lane_src(), axis=-1)
    return w.reshape(lead + (heads * HEAD_PAD,))


def _layer_weights(i, w):
    s0 = Q_RANK
    s1 = s0 + KV_RANK
    s2 = s1 + QK_ROPE_DIM
    s3 = s2 + GM_WIDTH
    w_in = w['w_in'][i]
    kr_cols = _lay_heads(jnp.pad(w_in[:, s1:s2], ((0, 0), (QK_NOPE_DIM, 0))), 1)
    w_in_l = jnp.concatenate([w_in[:, :s1], kr_cols, w_in[:, s2:s3], w_in[:, s3:]], axis=1)
    w_ukv = w['w_ukv'][i].reshape(KV_RANK, MLA_HEADS, QK_NOPE_DIM + V_HEAD_DIM)
    w_uk = jnp.pad(w_ukv[:, :, :QK_NOPE_DIM], ((0, 0), (0, 0), (0, QK_ROPE_DIM)))
    w_uk = _lay_heads(w_uk.reshape(KV_RANK, MLA_HEADS * QK_DIM), MLA_HEADS)
    w_uv = w_ukv[:, :, QK_NOPE_DIM:].reshape(KV_RANK, MLA_WIDTH)
    w_r = jnp.concatenate([w['w_router_expert'][i], w['w_router_group'][i]], axis=1)
    w_r = jnp.pad(w_r, ((0, 0), (0, LANES - w_r.shape[1])))
    b_s = jnp.repeat(w['b_s'][i].T, GM_GROUP_DIM, axis=1)
    return {
        'g_mix': w['g_mix_norm'][i].reshape(1, -1),
        'w_in': w_in_l.astype(BF16),
        'g_cq': w['g_cq'][i].reshape(1, -1),
        'g_ckv': w['g_ckv'][i].reshape(1, -1),
        'w_uq': _lay_heads(w['w_uq'][i], MLA_HEADS).astype(BF16),
        'w_uk': w_uk.astype(BF16),
        'w_uv': w_uv.astype(BF16),
        'g_q': _lay_heads(w['g_qn'][i] * (QK_DIM ** -0.5), 1).reshape(1, HEAD_PAD),
        'g_k': _lay_heads(w['g_kn'][i], 1).reshape(1, HEAD_PAD),
        'g_v': w['g_v'][i].reshape(1, GM_WIDTH),
        'w_s': w['w_s'][i],
        'b_s': b_s,
        'g_oa': w['g_out_mla'][i].reshape(1, -1),
        'g_og': w['g_out_gmlp'][i].reshape(1, -1),
        'w_out': w['w_out'][i].astype(BF16),
        'g_ffn': w['g_ffn_norm'][i].reshape(1, -1),
        'w_r': w_r.astype(BF16),
        'g_ple': w['g_ple'][i].reshape(1, -1),
        'w_gate': w['w_ple_gate'][i].astype(BF16),
        'b_gate': w['b_ple_gate'][i].reshape(1, -1),
        'w_ple': w['w_ple'][i].astype(BF16),
    }


def _rope_tables(seq):
    inv = ROPE_THETA ** (-jnp.arange(0, QK_ROPE_DIM, 2, dtype=F32) / QK_ROPE_DIM)
    ang = jnp.arange(seq, dtype=F32)[:, None] * inv[None, :]
    cos, sin = jnp.cos(ang), jnp.sin(ang)
    rc = jnp.concatenate([jnp.ones((seq, QK_NOPE_DIM), F32), cos, cos], axis=1)
    rs = jnp.concatenate([jnp.zeros((seq, QK_NOPE_DIM), F32), -sin, sin], axis=1)
    return _lay_heads(rc, 1), _lay_heads(rs, 1)


def _slot_rows(T):
    worst = TOP_K * T + (T // TM_MOE) * N_EXPERTS * (SUBLANES - 1) + N_EXPERTS * (BM - SUBLANES)
    return -(-worst // BM) * BM


def _moe_plan(cnt_mix, T):
    nt = T // TM_MOE
    i32 = jnp.int32
    cnt = cnt_mix[:, 0, :N_EXPERTS].astype(i32).reshape(nt, TM_MOE // TM_MIX, N_EXPERTS).sum(axis=1)
    run = (cnt + SUBLANES - 1) // SUBLANES * SUBLANES
    lend = jnp.cumsum(run, axis=1)
    lstart = lend - run
    rows_e = jnp.sum(run, axis=0)
    padded = (rows_e + BM - 1) // BM * BM
    p_end = jnp.cumsum(padded)
    p_start = p_end - padded
    gbase = p_start[None, :] + jnp.cumsum(run, axis=0) - run
    c8 = jnp.arange(N_CHUNKS, dtype=i32) * SUBLANES
    e_c = jnp.minimum(jnp.sum(c8[None, :, None] >= lend[:, None, :], axis=2), N_EXPERTS - 1)
    cdst = jnp.take_along_axis(gbase, e_c, axis=1) + c8[None, :] - jnp.take_along_axis(lstart, e_c, axis=1)
    cdst = jnp.where(c8[None, :] < lend[:, -1:], cdst, 0)
    n_blocks = _slot_rows(T) // BM
    block_start = jnp.arange(n_blocks, dtype=i32) * BM
    block_expert = jnp.minimum(jnp.sum(block_start[:, None] >= p_end[None, :], axis=1), N_EXPERTS - 1)
    lstart_f = jnp.pad(lstart.astype(F32), ((0, 0), (0, LANES - N_EXPERTS)))
    return {
        'cdst': cdst.reshape(-1).astype(i32),
        'nch': (lend[:, -1] // SUBLANES).astype(i32),
        'zstart': (p_start + rows_e).astype(i32),
        'nz': ((padded - rows_e) // SUBLANES).astype(i32),
        'lstart': jnp.broadcast_to(lstart_f[:, None, :], (nt, SUBLANES, LANES)),
        'block_expert': block_expert.astype(i32),
        'n_used': (p_end[-1] // BM).astype(i32).reshape(1),
    }


def kernel(x, p, g_mix_norm, w_in, g_cq, g_ckv, w_uq, w_ukv, g_qn, g_kn, g_v, w_s, b_s, g_out_mla, g_out_gmlp,
           w_out, g_ffn_norm, w_router_group, w_router_expert, w1, w3, w2, g_ple, w_ple_gate, b_ple_gate, w_ple):
    w = dict(g_mix_norm=g_mix_norm, w_in=w_in, g_cq=g_cq, g_ckv=g_ckv, w_uq=w_uq, w_ukv=w_ukv, g_qn=g_qn,
             g_kn=g_kn, g_v=g_v, w_s=w_s, b_s=b_s, g_out_mla=g_out_mla, g_out_gmlp=g_out_gmlp, w_out=w_out,
             g_ffn_norm=g_ffn_norm, w_router_group=w_router_group, w_router_expert=w_router_expert,
             w1=w1, w3=w3, w2=w2, g_ple=g_ple, w_ple_gate=w_ple_gate, b_ple_gate=b_ple_gate, w_ple=w_ple)
    batch, seq, d = x.shape
    depth = p.shape[0]
    T = batch * seq
    rope = _rope_tables(seq)
    h = x.reshape(T, d)
    for i in range(depth):
        lw = _layer_weights(i, w)
        q, k, v, u, vn = _inproj(h, lw, rope, seq)
        a = _attention(q, k, v, batch, seq)
        h1, xn, route, cnt = _mix(h, a, u, vn, lw)
        plan = _moe_plan(cnt, T)
        x_slots, pos = _dispatch(route, xn, plan, _slot_rows(T))
        y_slots = _experts(x_slots, plan, w1, w3, w2, i)
        h = _ple(h1, y_slots, pos, route, p[i].reshape(T, PLE_DIM), plan, lw)
    return h.reshape(batch, seq, d)
```

```python
import functools

import jax
import jax.numpy as jnp
from jax import lax
from jax.experimental import pallas as pl
from jax.experimental.pallas import tpu as pltpu

D_MODEL = 1024
CHUNK = 64
MLA_HEADS = 8
QK_NOPE_DIM = 64
QK_ROPE_DIM = 32
QK_DIM = QK_NOPE_DIM + QK_ROPE_DIM
V_HEAD_DIM = 64
Q_RANK = 256
KV_RANK = 128
ROPE_THETA = 10000.0
MLA_WIDTH = MLA_HEADS * V_HEAD_DIM
GM_GROUPS = 8
GM_GROUP_DIM = 64
GM_WIDTH = GM_GROUPS * GM_GROUP_DIM
GM_BLOCK = 128
N_EXPERT_GROUPS = 4
EXPERTS_PER_GROUP = 8
N_EXPERTS = N_EXPERT_GROUPS * EXPERTS_PER_GROUP
TOP_K = 2
D_EXPERT = 512
PLE_DIM = 256
RMS_EPS = 1e-6

LANES = 128
HEAD_PAD = LANES
QK_WIDTH = MLA_HEADS * HEAD_PAD
ROPE_HALF = QK_ROPE_DIM // 2
COL_CQ = 0
COL_CKV = COL_CQ + Q_RANK
COL_KR = COL_CKV + KV_RANK
COL_U = COL_KR + HEAD_PAD
COL_V = COL_U + GM_WIDTH
IN_COLS = COL_V + GM_WIDTH

SUBLANES = 8

TM_IN = 512
SUB_IN = 256
TQ = 256
ATTN_UNROLL_TILES = 3
TM_MIX = 512
BM = 512
TM_MOE = 512
SORTED_ROWS = -(-(TOP_K * TM_MOE + N_EXPERTS * (SUBLANES - 1)) // LANES) * LANES
N_CHUNKS = SORTED_ROWS // SUBLANES
VMEM_LIMIT = 48 * 1024 * 1024

F32 = jnp.float32
BF16 = jnp.bfloat16
NEG = float(jnp.finfo(jnp.float32).min)


def _dot(a, b):
    return jnp.dot(a, b, preferred_element_type=F32)


def _rms(x, g, width):
    ss = jnp.sum(x * x, axis=-1, keepdims=True) * (1.0 / width)
    return (x * lax.rsqrt(ss + RMS_EPS)) * g


def _inproj_kernel(h_ref, gmix_ref, win_ref, gcq_ref, gckv_ref, wuq_ref, wuk_ref, wuv_ref,
                   gq_ref, gk_ref, gv_ref, rc_ref, rs_ref,
                   q_ref, k_ref, v_ref, u_ref, vn_ref):
    gq, gk, gv = gq_ref[...], gk_ref[...], gv_ref[...]
    lane = lax.broadcasted_iota(jnp.int32, (1, LANES), 1)
    low = lane < GM_GROUP_DIM
    for r in range(TM_IN // SUB_IN):
        rows = slice(r * SUB_IN, (r + 1) * SUB_IN)
        hn = _rms(h_ref[rows, :], gmix_ref[...], D_MODEL).astype(BF16)
        proj = _dot(hn, win_ref[...])
        cq = proj[:, COL_CQ:COL_CQ + Q_RANK]
        ckv = proj[:, COL_CKV:COL_CKV + KV_RANK]
        kr = proj[:, COL_KR:COL_KR + HEAD_PAD]
        cqn = _rms(cq, gcq_ref[...], Q_RANK).astype(BF16)
        ckvn = _rms(ckv, gckv_ref[...], KV_RANK).astype(BF16)
        q_raw = _dot(cqn, wuq_ref[...])
        k_nope = _dot(ckvn, wuk_ref[...])
        v_ref[rows, :] = _dot(ckvn, wuv_ref[...]).astype(BF16)

        rc, rs = rc_ref[rows, :], rs_ref[rows, :]

        def rope(x):
            return x * rc + pltpu.roll(x, HEAD_PAD // 2, 1) * rs

        rk = rope(kr * gk)
        ss_kr = jnp.sum(kr * kr, axis=-1, keepdims=True)
        for hd in range(MLA_HEADS):
            sl = slice(hd * HEAD_PAD, (hd + 1) * HEAD_PAD)
            q_ref[rows, sl] = rope(_rms(q_raw[:, sl], gq, QK_DIM)).astype(BF16)
            kn = k_nope[:, sl]
            ms = (jnp.sum(kn * kn, axis=-1, keepdims=True) + ss_kr) * (1.0 / QK_DIM)
            k_ref[rows, sl] = ((kn * gk + rk) * lax.rsqrt(ms + RMS_EPS)).astype(BF16)

        u_ref[rows, :] = jax.nn.gelu(proj[:, COL_U:COL_U + GM_WIDTH])
        gvv = jax.nn.gelu(proj[:, COL_V:COL_V + GM_WIDTH])
        for j in range(GM_WIDTH // LANES):
            sl = slice(j * LANES, (j + 1) * LANES)
            x = gvv[:, sl]
            sq = x * x
            ss_lo = jnp.sum(jnp.where(low, sq, 0.0), axis=-1, keepdims=True)
            ss_hi = jnp.sum(jnp.where(low, 0.0, sq), axis=-1, keepdims=True)
            ms = jnp.where(low, ss_lo, ss_hi) * (1.0 / GM_GROUP_DIM)
            vn_ref[rows, sl] = ((x * lax.rsqrt(ms + RMS_EPS)) * gv[:, sl]).astype(BF16)


def _inproj(h, lw, rope, seq):
    T = h.shape[0]
    nt = T // TM_IN
    per_seq = seq // TM_IN
    tok = lambda w: pl.BlockSpec((TM_IN, w), lambda i: (i, 0))
    full = lambda a: pl.BlockSpec(a.shape, lambda i: (0,) * a.ndim)
    pos = pl.BlockSpec((TM_IN, LANES), lambda i: (i % per_seq, 0))
    consts = [lw['g_mix'], lw['w_in'], lw['g_cq'], lw['g_ckv'], lw['w_uq'], lw['w_uk'], lw['w_uv'],
              lw['g_q'], lw['g_k'], lw['g_v']]
    return pl.pallas_call(
        _inproj_kernel,
        grid=(nt,),
        in_specs=[tok(D_MODEL)] + [full(a) for a in consts] + [pos, pos],
        out_specs=[tok(QK_WIDTH), tok(QK_WIDTH), tok(MLA_WIDTH), tok(GM_WIDTH), tok(GM_WIDTH)],
        out_shape=[jax.ShapeDtypeStruct((T, QK_WIDTH), BF16), jax.ShapeDtypeStruct((T, QK_WIDTH), BF16),
                   jax.ShapeDtypeStruct((T, MLA_WIDTH), BF16), jax.ShapeDtypeStruct((T, GM_WIDTH), F32),
                   jax.ShapeDtypeStruct((T, GM_WIDTH), BF16)],
        compiler_params=pltpu.CompilerParams(dimension_semantics=("parallel",), vmem_limit_bytes=VMEM_LIMIT),
        name="inproj",
    )(h, *consts, *rope)


def _attn_tile(q_ref, k_ref, v_ref, o_ref, t):
    l0 = t * TQ
    row_chunk = lax.broadcasted_iota(jnp.int32, (TQ, TQ), 0) // CHUNK
    col_chunk = lax.broadcasted_iota(jnp.int32, (TQ, TQ), 1) // CHUNK
    diag_mask = col_chunk <= row_chunk
    low = lax.broadcasted_iota(jnp.int32, (1, LANES), 1) < V_HEAD_DIM
    nt = (((1,), (1,)), ((), ()))

    def aligned(x):
        return x if isinstance(x, int) else pl.multiple_of(x, LANES)

    def head_pair(hp, carry):
        vcol = aligned(hp * LANES)
        v_diag = v_ref[l0:l0 + TQ, pl.ds(vcol, LANES)]
        outs = []
        for hh in range(2):
            col = aligned(hp * (2 * HEAD_PAD) + hh * HEAD_PAD)
            q = q_ref[:, pl.ds(col, HEAD_PAD)]
            s_d = lax.dot_general(q, k_ref[l0:l0 + TQ, pl.ds(col, HEAD_PAD)], nt, preferred_element_type=F32)
            s_d = jnp.where(diag_mask, s_d, NEG)
            m = jnp.max(s_d, axis=-1, keepdims=True)
            if t > 0:
                s_m = lax.dot_general(q, k_ref[0:l0, pl.ds(col, HEAD_PAD)], nt, preferred_element_type=F32)
                m = jnp.maximum(m, jnp.max(s_m, axis=-1, keepdims=True))
                p_m = jnp.exp(s_m - m)
                l = jnp.sum(p_m, axis=-1, keepdims=True)
                o = _dot(p_m.astype(BF16), v_ref[0:l0, pl.ds(vcol, LANES)])
            p_d = jnp.exp(s_d - m)
            l_d = jnp.sum(p_d, axis=-1, keepdims=True)
            o_d = _dot(p_d.astype(BF16), v_diag)
            if t > 0:
                l, o = l + l_d, o + o_d
            else:
                l, o = l_d, o_d
            outs.append(o * (1.0 / l))
        o_ref[:, pl.ds(vcol, LANES)] = jnp.where(low, outs[0], outs[1])
        return carry

    if t < ATTN_UNROLL_TILES:
        for hp in range(MLA_HEADS // 2):
            head_pair(hp, 0)
    else:
        lax.fori_loop(0, MLA_HEADS // 2, head_pair, 0)


def _attn_kernel(q_ref, k_ref, v_ref, o_ref):
    qi = pl.program_id(1)
    for t in range(k_ref.shape[0] // TQ):
        pl.when(qi == t)(functools.partial(_attn_tile, q_ref, k_ref, v_ref, o_ref, t))


def _attention(q, k, v, batch, seq):
    q3 = q.reshape(batch, seq, QK_WIDTH)
    k3 = k.reshape(batch, seq, QK_WIDTH)
    v3 = v.reshape(batch, seq, MLA_WIDTH)
    out = pl.pallas_call(
        _attn_kernel,
        grid=(batch, seq // TQ),
        in_specs=[pl.BlockSpec((None, TQ, QK_WIDTH), lambda b, i: (b, i, 0)),
                  pl.BlockSpec((None, seq, QK_WIDTH), lambda b, i: (b, 0, 0)),
                  pl.BlockSpec((None, seq, MLA_WIDTH), lambda b, i: (b, 0, 0))],
        out_specs=pl.BlockSpec((None, TQ, MLA_WIDTH), lambda b, i: (b, i, 0)),
        out_shape=jax.ShapeDtypeStruct((batch, seq, MLA_WIDTH), F32),
        compiler_params=pltpu.CompilerParams(dimension_semantics=("parallel", "arbitrary"),
                                             vmem_limit_bytes=VMEM_LIMIT),
        name="attn",
    )(q3, k3, v3)
    return out.reshape(batch * seq, MLA_WIDTH)


def _mix_kernel(h_ref, a_ref, u_ref, vn_ref, ws_ref, bs_ref, goa_ref, gog_ref, wout_ref, gffn_ref, wr_ref,
                h1_ref, xn_ref, route_ref, cnt_ref):
    t_chunk = lax.broadcasted_iota(jnp.int32, (GM_BLOCK, GM_BLOCK), 0) // CHUNK
    j_chunk = lax.broadcasted_iota(jnp.int32, (GM_BLOCK, GM_BLOCK), 1) // CHUNK
    w_mask = j_chunk <= t_chunk
    lane = lax.broadcasted_iota(jnp.int32, (1, LANES), 1)
    low = lane < GM_GROUP_DIM
    ws = [jnp.where(w_mask, ws_ref[g], 0.0).astype(BF16) for g in range(GM_GROUPS)]
    bs = bs_ref[...]
    m_blocks = []
    for nb in range(TM_MIX // GM_BLOCK):
        rows = slice(nb * GM_BLOCK, (nb + 1) * GM_BLOCK)
        cols = []
        for j in range(GM_WIDTH // LANES):
            sl = slice(j * LANES, (j + 1) * LANES)
            x = vn_ref[rows, sl]
            s = jnp.where(low, _dot(ws[2 * j], x), _dot(ws[2 * j + 1], x)) + bs[:, sl]
            cols.append(u_ref[rows, sl] * s)
        m_blocks.append(jnp.concatenate(cols, axis=-1))
    m = jnp.concatenate(m_blocks, axis=0)
    an = _rms(a_ref[...], goa_ref[...], MLA_WIDTH).astype(BF16)
    mn = _rms(m, gog_ref[...], GM_WIDTH).astype(BF16)
    wout = wout_ref[...]
    h1 = h_ref[...] + _dot(an, wout[:MLA_WIDTH]) + _dot(mn, wout[MLA_WIDTH:])
    h1_ref[...] = h1
    xn = _rms(h1, gffn_ref[...], D_MODEL).astype(BF16)
    xn_ref[...] = xn

    logits = _dot(xn, wr_ref[...])
    big = float(LANES)
    lane_f = lane.astype(F32)
    grp_f = (lane // EXPERTS_PER_GROUP).astype(F32)
    is_g = (lane >= N_EXPERTS) & (lane < N_EXPERTS + N_EXPERT_GROUPS)
    lg = jnp.where(is_g, logits, NEG)
    gmax = jnp.max(lg, axis=-1, keepdims=True)
    g_sel = jnp.min(jnp.where(is_g & (lg == gmax), lane_f - N_EXPERTS, big), axis=-1, keepdims=True)
    denom = jnp.sum(jnp.where(is_g, jnp.exp(lg - gmax), 0.0), axis=-1, keepdims=True)
    p_sel = 1.0 / denom
    in_grp = (lane < N_EXPERTS) & (grp_f == g_sel)
    le = jnp.where(in_grp, logits, NEG)
    t1 = jnp.max(le, axis=-1, keepdims=True)
    i1 = jnp.min(jnp.where(in_grp & (le == t1), lane_f, big), axis=-1, keepdims=True)
    in2 = in_grp & (lane_f != i1)
    le2 = jnp.where(in2, logits, NEG)
    t2 = jnp.max(le2, axis=-1, keepdims=True)
    i2 = jnp.min(jnp.where(in2 & (le2 == t2), lane_f, big), axis=-1, keepdims=True)
    e2 = jnp.exp(t2 - t1)
    tot = 1.0 + e2
    g1 = p_sel * (1.0 / tot)
    g2 = p_sel * (e2 / tot)
    route = jnp.where(lane == 0, i1,
                      jnp.where(lane == 1, i2,
                                jnp.where(lane == 2, g1, jnp.where(lane == 3, g2, 0.0))))
    route_ref[...] = route
    hits = jnp.where((lane_f == i1) | (lane_f == i2), 1.0, 0.0)
    cnt_ref[...] = jnp.broadcast_to(jnp.sum(hits, axis=0, keepdims=True), (SUBLANES, LANES))


def _mix(h, a, u, vn, lw):
    T = h.shape[0]
    nt = T // TM_MIX
    tok = lambda w: pl.BlockSpec((TM_MIX, w), lambda i: (i, 0))
    full = lambda x: pl.BlockSpec(x.shape, lambda i: (0,) * x.ndim)
    consts = [lw['w_s'], lw['b_s'], lw['g_oa'], lw['g_og'], lw['w_out'], lw['g_ffn'], lw['w_r']]
    return pl.pallas_call(
        _mix_kernel,
        grid=(nt,),
        in_specs=[tok(D_MODEL), tok(MLA_WIDTH), tok(GM_WIDTH), tok(GM_WIDTH)] + [full(x) for x in consts],
        out_specs=[tok(D_MODEL), tok(D_MODEL), tok(LANES), pl.BlockSpec((None, SUBLANES, LANES), lambda i: (i, 0, 0))],
        out_shape=[jax.ShapeDtypeStruct((T, D_MODEL), F32), jax.ShapeDtypeStruct((T, D_MODEL), BF16),
                   jax.ShapeDtypeStruct((T, LANES), F32), jax.ShapeDtypeStruct((nt, SUBLANES, LANES), F32)],
        compiler_params=pltpu.CompilerParams(dimension_semantics=("parallel",), vmem_limit_bytes=VMEM_LIMIT),
        name="mix",
    )(h, a, u, vn, *consts)


def _chunk_copy(src_ref, src_row, dst_ref, dst_row, sem):
    return pltpu.make_async_copy(src_ref.at[pl.ds(pl.multiple_of(src_row, SUBLANES), SUBLANES), :],
                                 dst_ref.at[pl.ds(pl.multiple_of(dst_row, SUBLANES), SUBLANES), :], sem)


def _dispatch_kernel(cdst_ref, nch_ref, zstart_ref, nz_ref, nused_ref, route_ref, xn_ref, lstart_ref,
                     xs_hbm, pos_ref, sorted_buf, zero_buf, sems, zsem):
    i = pl.program_id(0)
    nt = pl.num_programs(0)
    slot = i % 2
    route = route_ref[...]
    lane = lax.broadcasted_iota(jnp.int32, (1, LANES), 1)
    lane_f = lane.astype(F32)
    oh0 = lane_f == route[:, 0:1]
    oh1 = lane_f == route[:, 1:2]
    earlier = lax.broadcasted_iota(jnp.int32, (TM_MOE, TM_MOE), 1) < lax.broadcasted_iota(jnp.int32, (TM_MOE, TM_MOE), 0)
    tri = jnp.where(earlier, 1.0, 0.0).astype(BF16)
    oh0_f = jnp.where(oh0, 1.0, 0.0)
    c0 = _dot(tri, oh0_f.astype(BF16))
    c1 = _dot(tri, jnp.where(oh1, 1.0, 0.0).astype(BF16))
    tot0 = jnp.sum(oh0_f, axis=0, keepdims=True)
    ls = lstart_ref[0:1, :]
    pos0 = jnp.sum(jnp.where(oh0, ls + c0, 0.0), axis=-1, keepdims=True)
    pos1 = jnp.sum(jnp.where(oh1, ls + tot0 + c1, 0.0), axis=-1, keepdims=True)
    pos = jnp.where(lane == 0, pos0, jnp.where(lane == 1, pos1, 0.0))
    pos_ref[...] = pos
    pos_t = pos.T
    r0 = pos_t[0:1, :].astype(jnp.int32)
    r1 = pos_t[1:2, :].astype(jnp.int32)
    rid = lax.broadcasted_iota(jnp.int32, (SORTED_ROWS, TM_MOE), 0)
    perm = jnp.where((rid == r0) | (rid == r1), 1.0, 0.0).astype(BF16)
    sorted_buf[slot] = _dot(perm, xn_ref[...])

    def issue(c, carry):
        _chunk_copy(sorted_buf.at[slot], c * SUBLANES, xs_hbm, cdst_ref[i * N_CHUNKS + c], sems.at[slot]).start()
        return carry

    lax.fori_loop(0, nch_ref[i], issue, 0)

    @pl.when(i == 0)
    def _():
        zero_buf[...] = jnp.zeros_like(zero_buf)

        def per_expert(e, carry):
            def zissue(j, c2):
                _chunk_copy(zero_buf, 0, xs_hbm, zstart_ref[e] + j * SUBLANES, zsem).start()
                return c2

            def zwait(j, c2):
                _chunk_copy(zero_buf, 0, xs_hbm, 0, zsem).wait()
                return c2

            lax.fori_loop(0, nz_ref[e], zissue, 0)
            lax.fori_loop(0, nz_ref[e], zwait, 0)
            return carry

        lax.fori_loop(0, N_EXPERTS, per_expert, 0)

        def tail_copy(b):
            return pltpu.make_async_copy(zero_buf, xs_hbm.at[pl.ds(pl.multiple_of(b * BM, BM), BM), :], zsem)

        def tail_issue(b, carry):
            tail_copy(b).start()
            return carry

        def tail_wait(b, carry):
            tail_copy(b).wait()
            return carry

        n_blocks = xs_hbm.shape[0] // BM
        lax.fori_loop(nused_ref[0], n_blocks, tail_issue, 0)
        lax.fori_loop(nused_ref[0], n_blocks, tail_wait, 0)

    def drain(n, sl):
        def wait(c, carry):
            _chunk_copy(sorted_buf.at[sl], 0, xs_hbm, 0, sems.at[sl]).wait()
            return carry

        lax.fori_loop(0, n, wait, 0)

    @pl.when(i > 0)
    def _():
        drain(nch_ref[i - 1], 1 - slot)

    @pl.when(i == nt - 1)
    def _():
        drain(nch_ref[i], slot)


def _dispatch(route, xn, plan, n_slots):
    T = route.shape[0]
    nt = T // TM_MOE
    grid_spec = pltpu.PrefetchScalarGridSpec(
        num_scalar_prefetch=5,
        grid=(nt,),
        in_specs=[pl.BlockSpec((TM_MOE, LANES), lambda i, *_: (i, 0)),
                  pl.BlockSpec((TM_MOE, D_MODEL), lambda i, *_: (i, 0)),
                  pl.BlockSpec((None, SUBLANES, LANES), lambda i, *_: (i, 0, 0))],
        out_specs=[pl.BlockSpec(memory_space=pl.ANY),
                   pl.BlockSpec((TM_MOE, LANES), lambda i, *_: (i, 0))],
        scratch_shapes=[pltpu.VMEM((2, SORTED_ROWS, D_MODEL), F32), pltpu.VMEM((BM, D_MODEL), F32),
                        pltpu.SemaphoreType.DMA((2,)), pltpu.SemaphoreType.DMA(())],
    )
    return pl.pallas_call(
        _dispatch_kernel,
        grid_spec=grid_spec,
        out_shape=[jax.ShapeDtypeStruct((n_slots, D_MODEL), F32), jax.ShapeDtypeStruct((T, LANES), F32)],
        compiler_params=pltpu.CompilerParams(dimension_semantics=("arbitrary",), vmem_limit_bytes=VMEM_LIMIT),
        name="dispatch",
    )(plan['cdst'], plan['nch'], plan['zstart'], plan['nz'], plan['n_used'], route, xn, plan['lstart'])


def _expert_kernel(be_ref, nused_ref, x_ref, w1_ref, w3_ref, w2_ref, y_ref, w1_b, w3_b, w2_b):
    i = pl.program_id(0)

    @pl.when((i == 0) | (be_ref[i] != be_ref[jnp.maximum(i - 1, 0)]))
    def _():
        w1_b[...] = w1_ref[...].astype(BF16)
        w3_b[...] = w3_ref[...].astype(BF16)
        w2_b[...] = w2_ref[...].astype(BF16)

    @pl.when(i < nused_ref[0])
    def _():
        x = x_ref[...].astype(BF16)
        a = _dot(x, w1_b[...])
        b = _dot(x, w3_b[...])
        act = (jax.nn.silu(a) * b).astype(BF16)
        y_ref[...] = _dot(act, w2_b[...])

    @pl.when(i >= nused_ref[0])
    def _():
        y_ref[...] = jnp.zeros_like(y_ref)


def _experts(x_slots, plan, w1, w3, w2, layer):
    n_slots = x_slots.shape[0]
    grid_spec = pltpu.PrefetchScalarGridSpec(
        num_scalar_prefetch=2,
        grid=(n_slots // BM,),
        in_specs=[pl.BlockSpec((BM, D_MODEL), lambda i, be, nu: (jnp.minimum(i, nu[0] - 1), 0)),
                  pl.BlockSpec((None, None, D_MODEL, D_EXPERT), lambda i, be, nu: (layer, be[i], 0, 0)),
                  pl.BlockSpec((None, None, D_MODEL, D_EXPERT), lambda i, be, nu: (layer, be[i], 0, 0)),
                  pl.BlockSpec((None, None, D_EXPERT, D_MODEL), lambda i, be, nu: (layer, be[i], 0, 0))],
        out_specs=pl.BlockSpec((BM, D_MODEL), lambda i, be, nu: (i, 0)),
        scratch_shapes=[pltpu.VMEM((D_MODEL, D_EXPERT), BF16), pltpu.VMEM((D_MODEL, D_EXPERT), BF16),
                        pltpu.VMEM((D_EXPERT, D_MODEL), BF16)],
    )
    return pl.pallas_call(
        _expert_kernel,
        grid_spec=grid_spec,
        out_shape=jax.ShapeDtypeStruct((n_slots, D_MODEL), F32),
        compiler_params=pltpu.CompilerParams(dimension_semantics=("arbitrary",), vmem_limit_bytes=VMEM_LIMIT),
        name="experts",
    )(plan['block_expert'], plan['n_used'], x_slots, w1, w3, w2)


def _ple_kernel(cdst_ref, nch_ref, h_ref, pos_ref, route_ref, p_ref, gple_ref, wg_ref, bg_ref, wp_ref, y_hbm,
                o_ref, y_buf, sems):
    i = pl.program_id(0)
    nt = pl.num_programs(0)
    slot = i % 2

    def gather(tile, sl):
        def issue(c, carry):
            _chunk_copy(y_hbm, cdst_ref[tile * N_CHUNKS + c], y_buf.at[sl], c * SUBLANES, sems.at[sl]).start()
            return carry

        lax.fori_loop(0, nch_ref[tile], issue, 0)

    @pl.when(i == 0)
    def _():
        y_buf[...] = jnp.zeros_like(y_buf)
        gather(0, 0)

    @pl.when(i + 1 < nt)
    def _():
        gather(i + 1, 1 - slot)

    def wait(c, carry):
        _chunk_copy(y_hbm, 0, y_buf.at[slot], 0, sems.at[slot]).wait()
        return carry

    lax.fori_loop(0, nch_ref[i], wait, 0)

    route = route_ref[...]
    pos = pos_ref[...]
    row = lax.broadcasted_iota(jnp.int32, (1, SORTED_ROWS), 1).astype(F32)
    unsort = (jnp.where(row == pos[:, 0:1], route[:, 2:3], 0.0)
              + jnp.where(row == pos[:, 1:2], route[:, 3:4], 0.0)).astype(BF16)
    h2 = h_ref[...] + _dot(unsort, y_buf[slot].astype(BF16))
    hn = _rms(h2, gple_ref[...], D_MODEL).astype(BF16)
    gate = jax.nn.sigmoid(_dot(hn, wg_ref[...]) + bg_ref[...])
    pw = _dot(p_ref[...].astype(BF16), wp_ref[...])
    o_ref[...] = h2 + pw * gate


def _ple(h1, y_slots, pos, route, p, plan, lw):
    T = h1.shape[0]
    tok = lambda w: pl.BlockSpec((TM_MOE, w), lambda i, *_: (i, 0))
    full = lambda x: pl.BlockSpec(x.shape, lambda i, *_: (0,) * x.ndim)
    consts = [lw['g_ple'], lw['w_gate'], lw['b_gate'], lw['w_ple']]
    grid_spec = pltpu.PrefetchScalarGridSpec(
        num_scalar_prefetch=2,
        grid=(T // TM_MOE,),
        in_specs=[tok(D_MODEL), tok(LANES), tok(LANES), tok(PLE_DIM)] + [full(x) for x in consts]
                 + [pl.BlockSpec(memory_space=pl.ANY)],
        out_specs=tok(D_MODEL),
        scratch_shapes=[pltpu.VMEM((2, SORTED_ROWS, D_MODEL), F32), pltpu.SemaphoreType.DMA((2,))],
    )
    return pl.pallas_call(
        _ple_kernel,
        grid_spec=grid_spec,
        out_shape=jax.ShapeDtypeStruct((T, D_MODEL), F32),
        compiler_params=pltpu.CompilerParams(dimension_semantics=("arbitrary",), vmem_limit_bytes=VMEM_LIMIT),
        name="ple",
    )(plan['cdst'], plan['nch'], h1, pos, route, p, *consts, y_slots)


def _head_lane_src():
    src = [QK_DIM] * HEAD_PAD
    for j in range(ROPE_HALF):
        src[j] = QK_NOPE_DIM + j
        src[HEAD_PAD // 2 + j] = QK_NOPE_DIM + ROPE_HALF + j
    nope_a = HEAD_PAD // 2 - ROPE_HALF
    for j in range(nope_a):
        src[ROPE_HALF + j] = j
    for j in range(QK_NOPE_DIM - nope_a):
        src[HEAD_PAD // 2 + ROPE_HALF + j] = nope_a + j
    return jnp.asarray(src, dtype=jnp.int32)


def _lay_heads(w, heads):
    lead = w.shape[:-1]
    w = w.reshape(lead + (heads, QK_DIM))
    w = jnp.pad(w, [(0, 0)] * (len(lead) + 1) + [(0, 1)])
    w = jnp.take(w, _head_lane_src(), axis=-1)
    return w.reshape(lead + (heads * HEAD_PAD,))


def _layer_weights(i, w):
    s0 = Q_RANK
    s1 = s0 + KV_RANK
    s2 = s1 + QK_ROPE_DIM
    s3 = s2 + GM_WIDTH
    w_in = w['w_in'][i]
    kr_cols = _lay_heads(jnp.pad(w_in[:, s1:s2], ((0, 0), (QK_NOPE_DIM, 0))), 1)
    w_in_l = jnp.concatenate([w_in[:, :s1], kr_cols, w_in[:, s2:s3], w_in[:, s3:]], axis=1)
    w_ukv = w['w_ukv'][i].reshape(KV_RANK, MLA_HEADS, QK_NOPE_DIM + V_HEAD_DIM)
    w_uk = jnp.pad(w_ukv[:, :, :QK_NOPE_DIM], ((0, 0), (0, 0), (0, QK_ROPE_DIM)))
    w_uk = _lay_heads(w_uk.reshape(KV_RANK, MLA_HEADS * QK_DIM), MLA_HEADS)
    w_uv = w_ukv[:, :, QK_NOPE_DIM:].reshape(KV_RANK, MLA_WIDTH)
    w_r = jnp.concatenate([w['w_router_expert'][i], w['w_router_group'][i]], axis=1)
    w_r = jnp.pad(w_r, ((0, 0), (0, LANES - w_r.shape[1])))
    b_s = jnp.repeat(w['b_s'][i].T, GM_GROUP_DIM, axis=1)
    return {
        'g_mix': w['g_mix_norm'][i].reshape(1, -1),
        'w_in': w_in_l.astype(BF16),
        'g_cq': w['g_cq'][i].reshape(1, -1),
        'g_ckv': w['g_ckv'][i].reshape(1, -1),
        'w_uq': _lay_heads(w['w_uq'][i], MLA_HEADS).astype(BF16),
        'w_uk': w_uk.astype(BF16),
        'w_uv': w_uv.astype(BF16),
        'g_q': _lay_heads(w['g_qn'][i] * (QK_DIM ** -0.5), 1).reshape(1, HEAD_PAD),
        'g_k': _lay_heads(w['g_kn'][i], 1).reshape(1, HEAD_PAD),
        'g_v': w['g_v'][i].reshape(1, GM_WIDTH),
        'w_s': w['w_s'][i],
        'b_s': b_s,
        'g_oa': w['g_out_mla'][i].reshape(1, -1),
        'g_og': w['g_out_gmlp'][i].reshape(1, -1),
        'w_out': w['w_out'][i].astype(BF16),
        'g_ffn': w['g_ffn_norm'][i].reshape(1, -1),
        'w_r': w_r.astype(BF16),
        'g_ple': w['g_ple'][i].reshape(1, -1),
        'w_gate': w['w_ple_gate'][i].astype(BF16),
        'b_gate': w['b_ple_gate'][i].reshape(1, -1),
        'w_ple': w['w_ple'][i].astype(BF16),
    }


def _rope_tables(seq):
    inv = ROPE_THETA ** (-jnp.arange(0, QK_ROPE_DIM, 2, dtype=F32) / QK_ROPE_DIM)
    ang = jnp.arange(seq, dtype=F32)[:, None] * inv[None, :]
    cos, sin = jnp.cos(ang), jnp.sin(ang)
    rc = jnp.concatenate([jnp.ones((seq, QK_NOPE_DIM), F32), cos, cos], axis=1)
    rs = jnp.concatenate([jnp.zeros((seq, QK_NOPE_DIM), F32), -sin, sin], axis=1)
    return _lay_heads(rc, 1), _lay_heads(rs, 1)


def _slot_rows(T):
    worst = TOP_K * T + (T // TM_MOE) * N_EXPERTS * (SUBLANES - 1) + N_EXPERTS * (BM - SUBLANES)
    return -(-worst // BM) * BM


def _moe_plan(cnt_mix, T):
    nt = T // TM_MOE
    i32 = jnp.int32
    cnt = cnt_mix[:, 0, :N_EXPERTS].astype(i32).reshape(nt, TM_MOE // TM_MIX, N_EXPERTS).sum(axis=1)
    run = (cnt + SUBLANES - 1) // SUBLANES * SUBLANES
    lend = jnp.cumsum(run, axis=1)
    lstart = lend - run
    rows_e = jnp.sum(run, axis=0)
    padded = (rows_e + BM - 1) // BM * BM
    p_end = jnp.cumsum(padded)
    p_start = p_end - padded
    gbase = p_start[None, :] + jnp.cumsum(run, axis=0) - run
    c8 = jnp.arange(N_CHUNKS, dtype=i32)[None, :, None] * SUBLANES
    in_run = (c8 >= lstart[:, None, :]) & (c8 < lend[:, None, :])
    cdst = jnp.sum(jnp.where(in_run, gbase[:, None, :] + c8 - lstart[:, None, :], 0), axis=2)
    n_blocks = _slot_rows(T) // BM
    block_start = jnp.arange(n_blocks, dtype=i32) * BM
    block_expert = jnp.minimum(jnp.sum(block_start[:, None] >= p_end[None, :], axis=1), N_EXPERTS - 1)
    lstart_f = jnp.pad(lstart.astype(F32), ((0, 0), (0, LANES - N_EXPERTS)))
    return {
        'cdst': cdst.reshape(-1).astype(i32),
        'nch': (lend[:, -1] // SUBLANES).astype(i32),
        'zstart': (p_start + rows_e).astype(i32),
        'nz': ((padded - rows_e) // SUBLANES).astype(i32),
        'lstart': jnp.broadcast_to(lstart_f[:, None, :], (nt, SUBLANES, LANES)),
        'block_expert': block_expert.astype(i32),
        'n_used': (p_end[-1] // BM).astype(i32).reshape(1),
    }


def kernel(x, p, g_mix_norm, w_in, g_cq, g_ckv, w_uq, w_ukv, g_qn, g_kn, g_v, w_s, b_s, g_out_mla, g_out_gmlp,
           w_out, g_ffn_norm, w_router_group, w_router_expert, w1, w3, w2, g_ple, w_ple_gate, b_ple_gate, w_ple):
    w = dict(g_mix_norm=g_mix_norm, w_in=w_in, g_cq=g_cq, g_ckv=g_ckv, w_uq=w_uq, w_ukv=w_ukv, g_qn=g_qn,
             g_kn=g_kn, g_v=g_v, w_s=w_s, b_s=b_s, g_out_mla=g_out_mla, g_out_gmlp=g_out_gmlp, w_out=w_out,
             g_ffn_norm=g_ffn_norm, w_router_group=w_router_group, w_router_expert=w_router_expert,
             w1=w1, w3=w3, w2=w2, g_ple=g_ple, w_ple_gate=w_ple_gate, b_ple_gate=b_ple_gate, w_ple=w_ple)
    batch, seq, d = x.shape
    depth = p.shape[0]
    T = batch * seq
    rope = _rope_tables(seq)
    h = x.reshape(T, d)
    for i in range(depth):
        lw = _layer_weights(i, w)
        q, k, v, u, vn = _inproj(h, lw, rope, seq)
        a = _attention(q, k, v, batch, seq)
        h1, xn, route, cnt = _mix(h, a, u, vn, lw)
        plan = _moe_plan(cnt, T)
        x_slots, pos = _dispatch(route, xn, plan, _slot_rows(T))
        y_slots = _experts(x_slots, plan, w1, w3, w2, i)
        h = _ple(h1, y_slots, pos, route, p[i].reshape(T, PLE_DIM), plan, lw)
    return h.reshape(batch, seq, d)
```

```python
import functools

import jax
import jax.numpy as jnp
from jax import lax
from jax.experimental import pallas as pl
from jax.experimental.pallas import tpu as pltpu

D_MODEL = 1024
CHUNK = 64
MLA_HEADS = 8
QK_NOPE_DIM = 64
QK_ROPE_DIM = 32
QK_DIM = QK_NOPE_DIM + QK_ROPE_DIM
V_HEAD_DIM = 64
Q_RANK = 256
KV_RANK = 128
ROPE_THETA = 10000.0
MLA_WIDTH = MLA_HEADS * V_HEAD_DIM
GM_GROUPS = 8
GM_GROUP_DIM = 64
GM_WIDTH = GM_GROUPS * GM_GROUP_DIM
GM_BLOCK = 128
N_EXPERT_GROUPS = 4
EXPERTS_PER_GROUP = 8
N_EXPERTS = N_EXPERT_GROUPS * EXPERTS_PER_GROUP
TOP_K = 2
D_EXPERT = 512
PLE_DIM = 256
RMS_EPS = 1e-6

LANES = 128
HEAD_PAD = LANES
QK_WIDTH = MLA_HEADS * HEAD_PAD
ROPE_HALF = QK_ROPE_DIM // 2
COL_CQ = 0
COL_CKV = COL_CQ + Q_RANK
COL_KR = COL_CKV + KV_RANK
COL_U = COL_KR + HEAD_PAD
COL_V = COL_U + GM_WIDTH
IN_COLS = COL_V + GM_WIDTH

SUBLANES = 8

TM_IN = 512
SUB_IN = 256
TQ = 256
ATTN_UNROLL_TILES = 8
TM_MIX = 512
BM = 512
TM_MOE = 512
SORTED_ROWS = -(-(TOP_K * TM_MOE + N_EXPERTS * (SUBLANES - 1)) // LANES) * LANES
N_CHUNKS = SORTED_ROWS // SUBLANES
VMEM_LIMIT = 48 * 1024 * 1024

F32 = jnp.float32
BF16 = jnp.bfloat16
NEG = float(jnp.finfo(jnp.float32).min)
MASKED = -1e30
LOG2_E = 1.4426950408889634


def _dot(a, b):
    return jnp.dot(a, b, preferred_element_type=F32)


def _rms(x, g, width):
    ss = jnp.sum(x * x, axis=-1, keepdims=True) * (1.0 / width)
    return (x * lax.rsqrt(ss + RMS_EPS)) * g


def _inproj_kernel(h_ref, gmix_ref, win_ref, gcq_ref, gckv_ref, wuq_ref, wuk_ref, wuv_ref,
                   gq_ref, gk_ref, gv_ref, rc_ref, rs_ref,
                   q_ref, k_ref, v_ref, u_ref, vn_ref):
    gq, gk, gv = gq_ref[...], gk_ref[...], gv_ref[...]
    lane = lax.broadcasted_iota(jnp.int32, (1, LANES), 1)
    low = lane < GM_GROUP_DIM
    for r in range(TM_IN // SUB_IN):
        rows = slice(r * SUB_IN, (r + 1) * SUB_IN)
        hn = _rms(h_ref[rows, :], gmix_ref[...], D_MODEL).astype(BF16)
        proj = _dot(hn, win_ref[...])
        cq = proj[:, COL_CQ:COL_CQ + Q_RANK]
        ckv = proj[:, COL_CKV:COL_CKV + KV_RANK]
        kr = proj[:, COL_KR:COL_KR + HEAD_PAD]
        cqn = _rms(cq, gcq_ref[...], Q_RANK).astype(BF16)
        ckvn = _rms(ckv, gckv_ref[...], KV_RANK).astype(BF16)
        q_raw = _dot(cqn, wuq_ref[...])
        k_nope = _dot(ckvn, wuk_ref[...])
        v_ref[rows, :] = _dot(ckvn, wuv_ref[...]).astype(BF16)

        rc, rs = rc_ref[rows, :], rs_ref[rows, :]

        def rope(x):
            return x * rc + pltpu.roll(x, HEAD_PAD // 2, 1) * rs

        rk = rope(kr * gk)
        ss_kr = jnp.sum(kr * kr, axis=-1, keepdims=True)
        for hd in range(MLA_HEADS):
            sl = slice(hd * HEAD_PAD, (hd + 1) * HEAD_PAD)
            q_ref[rows, sl] = rope(_rms(q_raw[:, sl], gq, QK_DIM)).astype(BF16)
            kn = k_nope[:, sl]
            ms = (jnp.sum(kn * kn, axis=-1, keepdims=True) + ss_kr) * (1.0 / QK_DIM)
            k_ref[rows, sl] = ((kn * gk + rk) * lax.rsqrt(ms + RMS_EPS)).astype(BF16)

        u_ref[rows, :] = jax.nn.gelu(proj[:, COL_U:COL_U + GM_WIDTH])
        gvv = jax.nn.gelu(proj[:, COL_V:COL_V + GM_WIDTH])
        for j in range(GM_WIDTH // LANES):
            sl = slice(j * LANES, (j + 1) * LANES)
            x = gvv[:, sl]
            sq = x * x
            ss_lo = jnp.sum(jnp.where(low, sq, 0.0), axis=-1, keepdims=True)
            ss_hi = jnp.sum(jnp.where(low, 0.0, sq), axis=-1, keepdims=True)
            ms = jnp.where(low, ss_lo, ss_hi) * (1.0 / GM_GROUP_DIM)
            vn_ref[rows, sl] = ((x * lax.rsqrt(ms + RMS_EPS)) * gv[:, sl]).astype(BF16)


def _inproj(h, lw, rope, seq):
    T = h.shape[0]
    nt = T // TM_IN
    per_seq = seq // TM_IN
    tok = lambda w: pl.BlockSpec((TM_IN, w), lambda i: (i, 0))
    full = lambda a: pl.BlockSpec(a.shape, lambda i: (0,) * a.ndim)
    pos = pl.BlockSpec((TM_IN, LANES), lambda i: (i % per_seq, 0))
    consts = [lw['g_mix'], lw['w_in'], lw['g_cq'], lw['g_ckv'], lw['w_uq'], lw['w_uk'], lw['w_uv'],
              lw['g_q'], lw['g_k'], lw['g_v']]
    return pl.pallas_call(
        _inproj_kernel,
        grid=(nt,),
        in_specs=[tok(D_MODEL)] + [full(a) for a in consts] + [pos, pos],
        out_specs=[tok(QK_WIDTH), tok(QK_WIDTH), tok(MLA_WIDTH), tok(GM_WIDTH), tok(GM_WIDTH)],
        out_shape=[jax.ShapeDtypeStruct((T, QK_WIDTH), BF16), jax.ShapeDtypeStruct((T, QK_WIDTH), BF16),
                   jax.ShapeDtypeStruct((T, MLA_WIDTH), BF16), jax.ShapeDtypeStruct((T, GM_WIDTH), F32),
                   jax.ShapeDtypeStruct((T, GM_WIDTH), BF16)],
        compiler_params=pltpu.CompilerParams(dimension_semantics=("parallel",), vmem_limit_bytes=VMEM_LIMIT),
        name="inproj",
    )(h, *consts, *rope)


def _attn_tile(q_ref, k_ref, vx_ref, o_ref, t):
    l0 = t * TQ
    row_chunk = lax.broadcasted_iota(jnp.int32, (TQ, TQ), 0) // CHUNK
    col_chunk = lax.broadcasted_iota(jnp.int32, (TQ, TQ), 1) // CHUNK
    diag_mask = col_chunk <= row_chunk
    low = lax.broadcasted_iota(jnp.int32, (1, LANES), 1) < V_HEAD_DIM
    nt = (((1,), (1,)), ((), ()))

    def aligned(x):
        return x if isinstance(x, int) else pl.multiple_of(x, LANES)

    def head_pair(hp, carry):
        ocol = aligned(hp * LANES)
        vcol = aligned(hp * (2 * LANES))
        v_diag = vx_ref[l0:l0 + TQ, pl.ds(vcol, 2 * LANES)]
        outs = []
        for hh in range(2):
            col = aligned(hp * (2 * HEAD_PAD) + hh * HEAD_PAD)
            q = q_ref[:, pl.ds(col, HEAD_PAD)]
            s_d = lax.dot_general(q, k_ref[l0:l0 + TQ, pl.ds(col, HEAD_PAD)], nt, preferred_element_type=F32)
            s_d = jnp.where(diag_mask, s_d, MASKED)
            m = jnp.max(s_d, axis=-1, keepdims=True)
            if t > 0:
                s_m = lax.dot_general(q, k_ref[0:l0, pl.ds(col, HEAD_PAD)], nt, preferred_element_type=F32)
                m = jnp.maximum(m, jnp.max(s_m, axis=-1, keepdims=True))
                o = _dot(jnp.exp2((s_m - m).astype(BF16)), vx_ref[0:l0, pl.ds(vcol, 2 * LANES)])
            o_d = _dot(jnp.exp2((s_d - m).astype(BF16)), v_diag)
            o = o + o_d if t > 0 else o_d
            outs.append(o[:, :LANES] * (1.0 / o[:, LANES:]))
        o_ref[:, pl.ds(ocol, LANES)] = jnp.where(low, outs[0], outs[1])
        return carry

    if t < ATTN_UNROLL_TILES:
        for hp in range(MLA_HEADS // 2):
            head_pair(hp, 0)
    else:
        lax.fori_loop(0, MLA_HEADS // 2, head_pair, 0)


def _attn_kernel(q_ref, k_ref, v_ref, o_ref, vx_ref):
    qi = pl.program_id(1)

    @pl.when(qi == 0)
    def _():
        ones = jnp.ones((v_ref.shape[0], LANES), BF16)
        for hp in range(MLA_HEADS // 2):
            vx_ref[:, hp * 2 * LANES:hp * 2 * LANES + LANES] = v_ref[:, hp * LANES:(hp + 1) * LANES]
            vx_ref[:, hp * 2 * LANES + LANES:(hp + 1) * 2 * LANES] = ones

    for t in range(k_ref.shape[0] // TQ):
        pl.when(qi == t)(functools.partial(_attn_tile, q_ref, k_ref, vx_ref, o_ref, t))


def _attention(q, k, v, batch, seq):
    q3 = q.reshape(batch, seq, QK_WIDTH)
    k3 = k.reshape(batch, seq, QK_WIDTH)
    v3 = v.reshape(batch, seq, MLA_WIDTH)
    out = pl.pallas_call(
        _attn_kernel,
        grid=(batch, seq // TQ),
        in_specs=[pl.BlockSpec((None, TQ, QK_WIDTH), lambda b, i: (b, i, 0)),
                  pl.BlockSpec((None, seq, QK_WIDTH), lambda b, i: (b, 0, 0)),
                  pl.BlockSpec((None, seq, MLA_WIDTH), lambda b, i: (b, 0, 0))],
        out_specs=pl.BlockSpec((None, TQ, MLA_WIDTH), lambda b, i: (b, i, 0)),
        out_shape=jax.ShapeDtypeStruct((batch, seq, MLA_WIDTH), F32),
        scratch_shapes=[pltpu.VMEM((seq, MLA_HEADS * LANES), BF16)],
        compiler_params=pltpu.CompilerParams(dimension_semantics=("arbitrary", "arbitrary"),
                                             vmem_limit_bytes=VMEM_LIMIT),
        name="attn",
    )(q3, k3, v3)
    return out.reshape(batch * seq, MLA_WIDTH)


def _mix_kernel(h_ref, a_ref, u_ref, vn_ref, ws_ref, bs_ref, goa_ref, gog_ref, wout_ref, gffn_ref, wr_ref,
                h1_ref, xn_ref, route_ref, cnt_ref):
    t_chunk = lax.broadcasted_iota(jnp.int32, (GM_BLOCK, GM_BLOCK), 0) // CHUNK
    j_chunk = lax.broadcasted_iota(jnp.int32, (GM_BLOCK, GM_BLOCK), 1) // CHUNK
    w_mask = j_chunk <= t_chunk
    lane = lax.broadcasted_iota(jnp.int32, (1, LANES), 1)
    low = lane < GM_GROUP_DIM
    ws = [jnp.where(w_mask, ws_ref[g], 0.0).astype(BF16) for g in range(GM_GROUPS)]
    bs = bs_ref[...]
    m_blocks = []
    for nb in range(TM_MIX // GM_BLOCK):
        rows = slice(nb * GM_BLOCK, (nb + 1) * GM_BLOCK)
        cols = []
        for j in range(GM_WIDTH // LANES):
            sl = slice(j * LANES, (j + 1) * LANES)
            x = vn_ref[rows, sl]
            s = jnp.where(low, _dot(ws[2 * j], x), _dot(ws[2 * j + 1], x)) + bs[:, sl]
            cols.append(u_ref[rows, sl] * s)
        m_blocks.append(jnp.concatenate(cols, axis=-1))
    m = jnp.concatenate(m_blocks, axis=0)
    an = _rms(a_ref[...], goa_ref[...], MLA_WIDTH).astype(BF16)
    mn = _rms(m, gog_ref[...], GM_WIDTH).astype(BF16)
    wout = wout_ref[...]
    h1 = h_ref[...] + _dot(an, wout[:MLA_WIDTH]) + _dot(mn, wout[MLA_WIDTH:])
    h1_ref[...] = h1
    xn = _rms(h1, gffn_ref[...], D_MODEL).astype(BF16)
    xn_ref[...] = xn

    logits = _dot(xn, wr_ref[...])
    big = float(LANES)
    lane_f = lane.astype(F32)
    grp_f = (lane // EXPERTS_PER_GROUP).astype(F32)
    is_g = (lane >= N_EXPERTS) & (lane < N_EXPERTS + N_EXPERT_GROUPS)
    lg = jnp.where(is_g, logits, NEG)
    gmax = jnp.max(lg, axis=-1, keepdims=True)
    g_sel = jnp.min(jnp.where(is_g & (lg == gmax), lane_f - N_EXPERTS, big), axis=-1, keepdims=True)
    denom = jnp.sum(jnp.where(is_g, jnp.exp(lg - gmax), 0.0), axis=-1, keepdims=True)
    p_sel = 1.0 / denom
    in_grp = (lane < N_EXPERTS) & (grp_f == g_sel)
    le = jnp.where(in_grp, logits, NEG)
    t1 = jnp.max(le, axis=-1, keepdims=True)
    i1 = jnp.min(jnp.where(in_grp & (le == t1), lane_f, big), axis=-1, keepdims=True)
    in2 = in_grp & (lane_f != i1)
    le2 = jnp.where(in2, logits, NEG)
    t2 = jnp.max(le2, axis=-1, keepdims=True)
    i2 = jnp.min(jnp.where(in2 & (le2 == t2), lane_f, big), axis=-1, keepdims=True)
    e2 = jnp.exp(t2 - t1)
    tot = 1.0 + e2
    g1 = p_sel * (1.0 / tot)
    g2 = p_sel * (e2 / tot)
    route = jnp.where(lane == 0, i1,
                      jnp.where(lane == 1, i2,
                                jnp.where(lane == 2, g1, jnp.where(lane == 3, g2, 0.0))))
    route_ref[...] = route
    hits = jnp.where((lane_f == i1) | (lane_f == i2), 1.0, 0.0)
    cnt_ref[...] = jnp.broadcast_to(jnp.sum(hits, axis=0, keepdims=True), (SUBLANES, LANES))


def _mix(h, a, u, vn, lw):
    T = h.shape[0]
    nt = T // TM_MIX
    tok = lambda w: pl.BlockSpec((TM_MIX, w), lambda i: (i, 0))
    full = lambda x: pl.BlockSpec(x.shape, lambda i: (0,) * x.ndim)
    consts = [lw['w_s'], lw['b_s'], lw['g_oa'], lw['g_og'], lw['w_out'], lw['g_ffn'], lw['w_r']]
    return pl.pallas_call(
        _mix_kernel,
        grid=(nt,),
        in_specs=[tok(D_MODEL), tok(MLA_WIDTH), tok(GM_WIDTH), tok(GM_WIDTH)] + [full(x) for x in consts],
        out_specs=[tok(D_MODEL), tok(D_MODEL), tok(LANES), pl.BlockSpec((None, SUBLANES, LANES), lambda i: (i, 0, 0))],
        out_shape=[jax.ShapeDtypeStruct((T, D_MODEL), F32), jax.ShapeDtypeStruct((T, D_MODEL), BF16),
                   jax.ShapeDtypeStruct((T, LANES), F32), jax.ShapeDtypeStruct((nt, SUBLANES, LANES), F32)],
        compiler_params=pltpu.CompilerParams(dimension_semantics=("parallel",), vmem_limit_bytes=VMEM_LIMIT),
        name="mix",
    )(h, a, u, vn, *consts)


def _chunk_copy(src_ref, src_row, dst_ref, dst_row, sem):
    return pltpu.make_async_copy(src_ref.at[pl.ds(pl.multiple_of(src_row, SUBLANES), SUBLANES), :],
                                 dst_ref.at[pl.ds(pl.multiple_of(dst_row, SUBLANES), SUBLANES), :], sem)


def _dispatch_kernel(cdst_ref, nch_ref, zstart_ref, nz_ref, nused_ref, route_ref, xn_ref, lstart_ref,
                     xs_hbm, pos_ref, sorted_buf, zero_buf, sems, zsem):
    i = pl.program_id(0)
    nt = pl.num_programs(0)
    slot = i % 2
    route = route_ref[...]
    lane = lax.broadcasted_iota(jnp.int32, (1, LANES), 1)
    lane_f = lane.astype(F32)
    oh0 = lane_f == route[:, 0:1]
    oh1 = lane_f == route[:, 1:2]
    earlier = lax.broadcasted_iota(jnp.int32, (TM_MOE, TM_MOE), 1) < lax.broadcasted_iota(jnp.int32, (TM_MOE, TM_MOE), 0)
    tri = jnp.where(earlier, 1.0, 0.0).astype(BF16)
    oh0_f = jnp.where(oh0, 1.0, 0.0)
    c0 = _dot(tri, oh0_f.astype(BF16))
    c1 = _dot(tri, jnp.where(oh1, 1.0, 0.0).astype(BF16))
    tot0 = jnp.sum(oh0_f, axis=0, keepdims=True)
    ls = lstart_ref[0:1, :]
    pos0 = jnp.sum(jnp.where(oh0, ls + c0, 0.0), axis=-1, keepdims=True)
    pos1 = jnp.sum(jnp.where(oh1, ls + tot0 + c1, 0.0), axis=-1, keepdims=True)
    pos = jnp.where(lane == 0, pos0, jnp.where(lane == 1, pos1, 0.0))
    pos_ref[...] = pos
    pos_t = pos.T
    r0 = pos_t[0:1, :].astype(jnp.int32)
    r1 = pos_t[1:2, :].astype(jnp.int32)
    rid = lax.broadcasted_iota(jnp.int32, (SORTED_ROWS, TM_MOE), 0)
    perm = jnp.where((rid == r0) | (rid == r1), 1.0, 0.0).astype(BF16)
    sorted_buf[slot] = _dot(perm, xn_ref[...])

    def issue(c, carry):
        _chunk_copy(sorted_buf.at[slot], c * SUBLANES, xs_hbm, cdst_ref[i * N_CHUNKS + c], sems.at[slot]).start()
        return carry

    lax.fori_loop(0, nch_ref[i], issue, 0)

    @pl.when(i == 0)
    def _():
        zero_buf[...] = jnp.zeros_like(zero_buf)

        def per_expert(e, carry):
            def zissue(j, c2):
                _chunk_copy(zero_buf, 0, xs_hbm, zstart_ref[e] + j * SUBLANES, zsem).start()
                return c2

            def zwait(j, c2):
                _chunk_copy(zero_buf, 0, xs_hbm, 0, zsem).wait()
                return c2

            lax.fori_loop(0, nz_ref[e], zissue, 0)
            lax.fori_loop(0, nz_ref[e], zwait, 0)
            return carry

        lax.fori_loop(0, N_EXPERTS, per_expert, 0)

        def tail_copy(b):
            return pltpu.make_async_copy(zero_buf, xs_hbm.at[pl.ds(pl.multiple_of(b * BM, BM), BM), :], zsem)

        def tail_issue(b, carry):
            tail_copy(b).start()
            return carry

        def tail_wait(b, carry):
            tail_copy(b).wait()
            return carry

        n_blocks = xs_hbm.shape[0] // BM
        lax.fori_loop(nused_ref[0], n_blocks, tail_issue, 0)
        lax.fori_loop(nused_ref[0], n_blocks, tail_wait, 0)

    def drain(n, sl):
        def wait(c, carry):
            _chunk_copy(sorted_buf.at[sl], 0, xs_hbm, 0, sems.at[sl]).wait()
            return carry

        lax.fori_loop(0, n, wait, 0)

    @pl.when(i > 0)
    def _():
        drain(nch_ref[i - 1], 1 - slot)

    @pl.when(i == nt - 1)
    def _():
        drain(nch_ref[i], slot)


def _dispatch(route, xn, plan, n_slots):
    T = route.shape[0]
    nt = T // TM_MOE
    grid_spec = pltpu.PrefetchScalarGridSpec(
        num_scalar_prefetch=5,
        grid=(nt,),
        in_specs=[pl.BlockSpec((TM_MOE, LANES), lambda i, *_: (i, 0)),
                  pl.BlockSpec((TM_MOE, D_MODEL), lambda i, *_: (i, 0)),
                  pl.BlockSpec((None, SUBLANES, LANES), lambda i, *_: (i, 0, 0))],
        out_specs=[pl.BlockSpec(memory_space=pl.ANY),
                   pl.BlockSpec((TM_MOE, LANES), lambda i, *_: (i, 0))],
        scratch_shapes=[pltpu.VMEM((2, SORTED_ROWS, D_MODEL), F32), pltpu.VMEM((BM, D_MODEL), F32),
                        pltpu.SemaphoreType.DMA((2,)), pltpu.SemaphoreType.DMA(())],
    )
    return pl.pallas_call(
        _dispatch_kernel,
        grid_spec=grid_spec,
        out_shape=[jax.ShapeDtypeStruct((n_slots, D_MODEL), F32), jax.ShapeDtypeStruct((T, LANES), F32)],
        compiler_params=pltpu.CompilerParams(dimension_semantics=("arbitrary",), vmem_limit_bytes=VMEM_LIMIT),
        name="dispatch",
    )(plan['cdst'], plan['nch'], plan['zstart'], plan['nz'], plan['n_used'], route, xn, plan['lstart'])


def _expert_kernel(be_ref, nused_ref, x_ref, w1_ref, w3_ref, w2_ref, y_ref, w1_b, w3_b, w2_b):
    i = pl.program_id(0)

    @pl.when((i == 0) | (be_ref[i] != be_ref[jnp.maximum(i - 1, 0)]))
    def _():
        w1_b[...] = w1_ref[...].astype(BF16)
        w3_b[...] = w3_ref[...].astype(BF16)
        w2_b[...] = w2_ref[...].astype(BF16)

    @pl.when(i < nused_ref[0])
    def _():
        x = x_ref[...].astype(BF16)
        a = _dot(x, w1_b[...])
        b = _dot(x, w3_b[...])
        act = (jax.nn.silu(a) * b).astype(BF16)
        y_ref[...] = _dot(act, w2_b[...])

    @pl.when(i >= nused_ref[0])
    def _():
        y_ref[...] = jnp.zeros_like(y_ref)


def _experts(x_slots, plan, w1, w3, w2, layer):
    n_slots = x_slots.shape[0]
    grid_spec = pltpu.PrefetchScalarGridSpec(
        num_scalar_prefetch=2,
        grid=(n_slots // BM,),
        in_specs=[pl.BlockSpec((BM, D_MODEL), lambda i, be, nu: (jnp.minimum(i, nu[0] - 1), 0)),
                  pl.BlockSpec((None, None, D_MODEL, D_EXPERT), lambda i, be, nu: (layer, be[i], 0, 0)),
                  pl.BlockSpec((None, None, D_MODEL, D_EXPERT), lambda i, be, nu: (layer, be[i], 0, 0)),
                  pl.BlockSpec((None, None, D_EXPERT, D_MODEL), lambda i, be, nu: (layer, be[i], 0, 0))],
        out_specs=pl.BlockSpec((BM, D_MODEL), lambda i, be, nu: (i, 0)),
        scratch_shapes=[pltpu.VMEM((D_MODEL, D_EXPERT), BF16), pltpu.VMEM((D_MODEL, D_EXPERT), BF16),
                        pltpu.VMEM((D_EXPERT, D_MODEL), BF16)],
    )
    return pl.pallas_call(
        _expert_kernel,
        grid_spec=grid_spec,
        out_shape=jax.ShapeDtypeStruct((n_slots, D_MODEL), F32),
        compiler_params=pltpu.CompilerParams(dimension_semantics=("arbitrary",), vmem_limit_bytes=VMEM_LIMIT),
        name="experts",
    )(plan['block_expert'], plan['n_used'], x_slots, w1, w3, w2)


def _ple_kernel(cdst_ref, nch_ref, h_ref, pos_ref, route_ref, p_ref, gple_ref, wg_ref, bg_ref, wp_ref, y_hbm,
                o_ref, y_buf, sems):
    i = pl.program_id(0)
    nt = pl.num_programs(0)
    slot = i % 2

    def gather(tile, sl):
        def issue(c, carry):
            _chunk_copy(y_hbm, cdst_ref[tile * N_CHUNKS + c], y_buf.at[sl], c * SUBLANES, sems.at[sl]).start()
            return carry

        lax.fori_loop(0, nch_ref[tile], issue, 0)

    @pl.when(i == 0)
    def _():
        y_buf[...] = jnp.zeros_like(y_buf)
        gather(0, 0)

    @pl.when(i + 1 < nt)
    def _():
        gather(i + 1, 1 - slot)

    def wait(c, carry):
        _chunk_copy(y_hbm, 0, y_buf.at[slot], 0, sems.at[slot]).wait()
        return carry

    lax.fori_loop(0, nch_ref[i], wait, 0)

    route = route_ref[...]
    pos = pos_ref[...]
    row = lax.broadcasted_iota(jnp.int32, (1, SORTED_ROWS), 1).astype(F32)
    unsort = (jnp.where(row == pos[:, 0:1], route[:, 2:3], 0.0)
              + jnp.where(row == pos[:, 1:2], route[:, 3:4], 0.0)).astype(BF16)
    h2 = h_ref[...] + _dot(unsort, y_buf[slot].astype(BF16))
    hn = _rms(h2, gple_ref[...], D_MODEL).astype(BF16)
    gate = jax.nn.sigmoid(_dot(hn, wg_ref[...]) + bg_ref[...])
    pw = _dot(p_ref[...].astype(BF16), wp_ref[...])
    o_ref[...] = h2 + pw * gate


def _ple(h1, y_slots, pos, route, p, plan, lw):
    T = h1.shape[0]
    tok = lambda w: pl.BlockSpec((TM_MOE, w), lambda i, *_: (i, 0))
    full = lambda x: pl.BlockSpec(x.shape, lambda i, *_: (0,) * x.ndim)
    consts = [lw['g_ple'], lw['w_gate'], lw['b_gate'], lw['w_ple']]
    grid_spec = pltpu.PrefetchScalarGridSpec(
        num_scalar_prefetch=2,
        grid=(T // TM_MOE,),
        in_specs=[tok(D_MODEL), tok(LANES), tok(LANES), tok(PLE_DIM)] + [full(x) for x in consts]
                 + [pl.BlockSpec(memory_space=pl.ANY)],
        out_specs=tok(D_MODEL),
        scratch_shapes=[pltpu.VMEM((2, SORTED_ROWS, D_MODEL), F32), pltpu.SemaphoreType.DMA((2,))],
    )
    return pl.pallas_call(
        _ple_kernel,
        grid_spec=grid_spec,
        out_shape=jax.ShapeDtypeStruct((T, D_MODEL), F32),
        compiler_params=pltpu.CompilerParams(dimension_semantics=("arbitrary",), vmem_limit_bytes=VMEM_LIMIT),
        name="ple",
    )(plan['cdst'], plan['nch'], h1, pos, route, p, *consts, y_slots)


def _head_lane_src():
    src = [QK_DIM] * HEAD_PAD
    for j in range(ROPE_HALF):
        src[j] = QK_NOPE_DIM + j
        src[HEAD_PAD // 2 + j] = QK_NOPE_DIM + ROPE_HALF + j
    nope_a = HEAD_PAD // 2 - ROPE_HALF
    for j in range(nope_a):
        src[ROPE_HALF + j] = j
    for j in range(QK_NOPE_DIM - nope_a):
        src[HEAD_PAD // 2 + ROPE_HALF + j] = nope_a + j
    return jnp.asarray(src, dtype=jnp.int32)


def _lay_heads(w, heads):
    lead = w.shape[:-1]
    w = w.reshape(lead + (heads, QK_DIM))
    w = jnp.pad(w, [(0, 0)] * (len(lead) + 1) + [(0, 1)])
    w = jnp.take(w, _head_lane_src(), axis=-1)
    return w.reshape(lead + (heads * HEAD_PAD,))


def _layer_weights(i, w):
    s0 = Q_RANK
    s1 = s0 + KV_RANK
    s2 = s1 + QK_ROPE_DIM
    s3 = s2 + GM_WIDTH
    w_in = w['w_in'][i]
    kr_cols = _lay_heads(jnp.pad(w_in[:, s1:s2], ((0, 0), (QK_NOPE_DIM, 0))), 1)
    w_in_l = jnp.concatenate([w_in[:, :s1], kr_cols, w_in[:, s2:s3], w_in[:, s3:]], axis=1)
    w_ukv = w['w_ukv'][i].reshape(KV_RANK, MLA_HEADS, QK_NOPE_DIM + V_HEAD_DIM)
    w_uk = jnp.pad(w_ukv[:, :, :QK_NOPE_DIM], ((0, 0), (0, 0), (0, QK_ROPE_DIM)))
    w_uk = _lay_heads(w_uk.reshape(KV_RANK, MLA_HEADS * QK_DIM), MLA_HEADS)
    w_uv = w_ukv[:, :, QK_NOPE_DIM:].reshape(KV_RANK, MLA_WIDTH)
    w_r = jnp.concatenate([w['w_router_expert'][i], w['w_router_group'][i]], axis=1)
    w_r = jnp.pad(w_r, ((0, 0), (0, LANES - w_r.shape[1])))
    b_s = jnp.repeat(w['b_s'][i].T, GM_GROUP_DIM, axis=1)
    return {
        'g_mix': w['g_mix_norm'][i].reshape(1, -1),
        'w_in': w_in_l.astype(BF16),
        'g_cq': w['g_cq'][i].reshape(1, -1),
        'g_ckv': w['g_ckv'][i].reshape(1, -1),
        'w_uq': _lay_heads(w['w_uq'][i], MLA_HEADS).astype(BF16),
        'w_uk': w_uk.astype(BF16),
        'w_uv': w_uv.astype(BF16),
        'g_q': _lay_heads(w['g_qn'][i] * (QK_DIM ** -0.5 * LOG2_E), 1).reshape(1, HEAD_PAD),
        'g_k': _lay_heads(w['g_kn'][i], 1).reshape(1, HEAD_PAD),
        'g_v': w['g_v'][i].reshape(1, GM_WIDTH),
        'w_s': w['w_s'][i],
        'b_s': b_s,
        'g_oa': w['g_out_mla'][i].reshape(1, -1),
        'g_og': w['g_out_gmlp'][i].reshape(1, -1),
        'w_out': w['w_out'][i].astype(BF16),
        'g_ffn': w['g_ffn_norm'][i].reshape(1, -1),
        'w_r': w_r.astype(BF16),
        'g_ple': w['g_ple'][i].reshape(1, -1),
        'w_gate': w['w_ple_gate'][i].astype(BF16),
        'b_gate': w['b_ple_gate'][i].reshape(1, -1),
        'w_ple': w['w_ple'][i].astype(BF16),
    }


def _rope_tables(seq):
    inv = ROPE_THETA ** (-jnp.arange(0, QK_ROPE_DIM, 2, dtype=F32) / QK_ROPE_DIM)
    ang = jnp.arange(seq, dtype=F32)[:, None] * inv[None, :]
    cos, sin = jnp.cos(ang), jnp.sin(ang)
    rc = jnp.concatenate([jnp.ones((seq, QK_NOPE_DIM), F32), cos, cos], axis=1)
    rs = jnp.concatenate([jnp.zeros((seq, QK_NOPE_DIM), F32), -sin, sin], axis=1)
    return _lay_heads(rc, 1), _lay_heads(rs, 1)


def _slot_rows(T):
    worst = TOP_K * T + (T // TM_MOE) * N_EXPERTS * (SUBLANES - 1) + N_EXPERTS * (BM - SUBLANES)
    return -(-worst // BM) * BM


def _moe_plan(cnt_mix, T):
    nt = T // TM_MOE
    i32 = jnp.int32
    cnt = cnt_mix[:, 0, :N_EXPERTS].astype(i32).reshape(nt, TM_MOE // TM_MIX, N_EXPERTS).sum(axis=1)
    run = (cnt + SUBLANES - 1) // SUBLANES * SUBLANES
    lend = jnp.cumsum(run, axis=1)
    lstart = lend - run
    rows_e = jnp.sum(run, axis=0)
    padded = (rows_e + BM - 1) // BM * BM
    p_end = jnp.cumsum(padded)
    p_start = p_end - padded
    gbase = p_start[None, :] + jnp.cumsum(run, axis=0) - run
    c8 = jnp.arange(N_CHUNKS, dtype=i32)[None, :, None] * SUBLANES
    in_run = (c8 >= lstart[:, None, :]) & (c8 < lend[:, None, :])
    cdst = jnp.sum(jnp.where(in_run, gbase[:, None, :] + c8 - lstart[:, None, :], 0), axis=2)
    n_blocks = _slot_rows(T) // BM
    block_start = jnp.arange(n_blocks, dtype=i32) * BM
    block_expert = jnp.minimum(jnp.sum(block_start[:, None] >= p_end[None, :], axis=1), N_EXPERTS - 1)
    lstart_f = jnp.pad(lstart.astype(F32), ((0, 0), (0, LANES - N_EXPERTS)))
    return {
        'cdst': cdst.reshape(-1).astype(i32),
        'nch': (lend[:, -1] // SUBLANES).astype(i32),
        'zstart': (p_start + rows_e).astype(i32),
        'nz': ((padded - rows_e) // SUBLANES).astype(i32),
        'lstart': jnp.broadcast_to(lstart_f[:, None, :], (nt, SUBLANES, LANES)),
        'block_expert': block_expert.astype(i32),
        'n_used': (p_end[-1] // BM).astype(i32).reshape(1),
    }


def kernel(x, p, g_mix_norm, w_in, g_cq, g_ckv, w_uq, w_ukv, g_qn, g_kn, g_v, w_s, b_s, g_out_mla, g_out_gmlp,
           w_out, g_ffn_norm, w_router_group, w_router_expert, w1, w3, w2, g_ple, w_ple_gate, b_ple_gate, w_ple):
    w = dict(g_mix_norm=g_mix_norm, w_in=w_in, g_cq=g_cq, g_ckv=g_ckv, w_uq=w_uq, w_ukv=w_ukv, g_qn=g_qn,
             g_kn=g_kn, g_v=g_v, w_s=w_s, b_s=b_s, g_out_mla=g_out_mla, g_out_gmlp=g_out_gmlp, w_out=w_out,
             g_ffn_norm=g_ffn_norm, w_router_group=w_router_group, w_router_expert=w_router_expert,
             w1=w1, w3=w3, w2=w2, g_ple=g_ple, w_ple_gate=w_ple_gate, b_ple_gate=b_ple_gate, w_ple=w_ple)
    batch, seq, d = x.shape
    depth = p.shape[0]
    T = batch * seq
    rope = _rope_tables(seq)
    h = x.reshape(T, d)
    for i in range(depth):
        lw = _layer_weights(i, w)
        q, k, v, u, vn = _inproj(h, lw, rope, seq)
        a = _attention(q, k, v, batch, seq)
        h1, xn, route, cnt = _mix(h, a, u, vn, lw)
        plan = _moe_plan(cnt, T)
        x_slots, pos = _dispatch(route, xn, plan, _slot_rows(T))
        y_slots = _experts(x_slots, plan, w1, w3, w2, i)
        h = _ple(h1, y_slots, pos, route, p[i].reshape(T, PLE_DIM), plan, lw)
    return h.reshape(batch, seq, d)
```

```python
import functools

import jax
import jax.numpy as jnp
from jax import lax
from jax.experimental import pallas as pl
from jax.experimental.pallas import tpu as pltpu

D_MODEL = 1024
CHUNK = 64
MLA_HEADS = 8
QK_NOPE_DIM = 64
QK_ROPE_DIM = 32
QK_DIM = QK_NOPE_DIM + QK_ROPE_DIM
V_HEAD_DIM = 64
Q_RANK = 256
KV_RANK = 128
ROPE_THETA = 10000.0
MLA_WIDTH = MLA_HEADS * V_HEAD_DIM
GM_GROUPS = 8
GM_GROUP_DIM = 64
GM_WIDTH = GM_GROUPS * GM_GROUP_DIM
GM_BLOCK = 128
N_EXPERT_GROUPS = 4
EXPERTS_PER_GROUP = 8
N_EXPERTS = N_EXPERT_GROUPS * EXPERTS_PER_GROUP
TOP_K = 2
D_EXPERT = 512
PLE_DIM = 256
RMS_EPS = 1e-6

LANES = 128
HEAD_PAD = LANES
QK_WIDTH = MLA_HEADS * HEAD_PAD
ROPE_HALF = QK_ROPE_DIM // 2
COL_CQ = 0
COL_CKV = COL_CQ + Q_RANK
COL_KR = COL_CKV + KV_RANK
COL_U = COL_KR + HEAD_PAD
COL_V = COL_U + GM_WIDTH
IN_COLS = COL_V + GM_WIDTH

SUBLANES = 8

TM_IN = 512
SUB_IN = 256
TQ = 256
ATTN_UNROLL_TILES = 8
TM_MIX = 512
BM = 512
TM_MOE = 512
SORTED_ROWS = -(-(TOP_K * TM_MOE + N_EXPERTS * (SUBLANES - 1)) // LANES) * LANES
N_CHUNKS = SORTED_ROWS // SUBLANES
VMEM_LIMIT = 48 * 1024 * 1024

F32 = jnp.float32
BF16 = jnp.bfloat16
NEG = float(jnp.finfo(jnp.float32).min)
MASKED = -1e30
LOG2_E = 1.4426950408889634


def _dot(a, b):
    return jnp.dot(a, b, preferred_element_type=F32)


def _rms(x, g, width):
    ss = jnp.sum(x * x, axis=-1, keepdims=True) * (1.0 / width)
    return (x * lax.rsqrt(ss + RMS_EPS)) * g


def _inproj_kernel(h_ref, gmix_ref, win_ref, gcq_ref, gckv_ref, wuq_ref, wuk_ref, wuv_ref,
                   gq_ref, gk_ref, gv_ref, rc_ref, rs_ref,
                   q_ref, k_ref, v_ref, u_ref, vn_ref):
    gq, gk, gv = gq_ref[...], gk_ref[...], gv_ref[...]
    lane = lax.broadcasted_iota(jnp.int32, (1, LANES), 1)
    low = lane < GM_GROUP_DIM
    staged = []
    for r in range(TM_IN // SUB_IN):
        rows = slice(r * SUB_IN, (r + 1) * SUB_IN)
        hn = _rms(h_ref[rows, :], gmix_ref[...], D_MODEL).astype(BF16)
        proj = _dot(hn, win_ref[...])
        cq = proj[:, COL_CQ:COL_CQ + Q_RANK]
        ckv = proj[:, COL_CKV:COL_CKV + KV_RANK]
        cqn = _rms(cq, gcq_ref[...], Q_RANK).astype(BF16)
        ckvn = _rms(ckv, gckv_ref[...], KV_RANK).astype(BF16)
        q_raw = _dot(cqn, wuq_ref[...])
        k_nope = _dot(ckvn, wuk_ref[...])
        v_ref[rows, :] = _dot(ckvn, wuv_ref[...]).astype(BF16)
        staged.append((rows, proj, q_raw, k_nope))

    for rows, proj, q_raw, k_nope in staged:
        kr = proj[:, COL_KR:COL_KR + HEAD_PAD]
        rc, rs = rc_ref[rows, :], rs_ref[rows, :]

        def rope(x):
            return x * rc + pltpu.roll(x, HEAD_PAD // 2, 1) * rs

        rk = rope(kr * gk)
        ss_kr = jnp.sum(kr * kr, axis=-1, keepdims=True)
        heads = [slice(hd * HEAD_PAD, (hd + 1) * HEAD_PAD) for hd in range(MLA_HEADS)]
        tiles = [slice(j * LANES, (j + 1) * LANES) for j in range(GM_WIDTH // LANES)]
        gvv = jax.nn.gelu(proj[:, COL_V:COL_V + GM_WIDTH])
        ss_q = [jnp.sum(q_raw[:, sl] * q_raw[:, sl], axis=-1, keepdims=True) for sl in heads]
        ss_k = [jnp.sum(k_nope[:, sl] * k_nope[:, sl], axis=-1, keepdims=True) + ss_kr for sl in heads]
        sq_v = [gvv[:, sl] * gvv[:, sl] for sl in tiles]
        ss_v = [jnp.where(low, jnp.sum(jnp.where(low, sq, 0.0), axis=-1, keepdims=True),
                          jnp.sum(jnp.where(low, 0.0, sq), axis=-1, keepdims=True)) for sq in sq_v]
        inv_q = [lax.rsqrt(ss * (1.0 / QK_DIM) + RMS_EPS) for ss in ss_q]
        inv_k = [lax.rsqrt(ss * (1.0 / QK_DIM) + RMS_EPS) for ss in ss_k]
        inv_v = [lax.rsqrt(ss * (1.0 / GM_GROUP_DIM) + RMS_EPS) for ss in ss_v]
        for sl, iq, ik in zip(heads, inv_q, inv_k):
            q_ref[rows, sl] = rope((q_raw[:, sl] * iq) * gq).astype(BF16)
            k_ref[rows, sl] = ((k_nope[:, sl] * gk + rk) * ik).astype(BF16)
        for sl, iv in zip(tiles, inv_v):
            vn_ref[rows, sl] = ((gvv[:, sl] * iv) * gv[:, sl]).astype(BF16)
        u_ref[rows, :] = jax.nn.gelu(proj[:, COL_U:COL_U + GM_WIDTH])


def _inproj(h, lw, rope, seq):
    T = h.shape[0]
    nt = T // TM_IN
    per_seq = seq // TM_IN
    tok = lambda w: pl.BlockSpec((TM_IN, w), lambda i: (i, 0))
    full = lambda a: pl.BlockSpec(a.shape, lambda i: (0,) * a.ndim)
    pos = pl.BlockSpec((TM_IN, LANES), lambda i: (i % per_seq, 0))
    consts = [lw['g_mix'], lw['w_in'], lw['g_cq'], lw['g_ckv'], lw['w_uq'], lw['w_uk'], lw['w_uv'],
              lw['g_q'], lw['g_k'], lw['g_v']]
    return pl.pallas_call(
        _inproj_kernel,
        grid=(nt,),
        in_specs=[tok(D_MODEL)] + [full(a) for a in consts] + [pos, pos],
        out_specs=[tok(QK_WIDTH), tok(QK_WIDTH), tok(MLA_WIDTH), tok(GM_WIDTH), tok(GM_WIDTH)],
        out_shape=[jax.ShapeDtypeStruct((T, QK_WIDTH), BF16), jax.ShapeDtypeStruct((T, QK_WIDTH), BF16),
                   jax.ShapeDtypeStruct((T, MLA_WIDTH), BF16), jax.ShapeDtypeStruct((T, GM_WIDTH), F32),
                   jax.ShapeDtypeStruct((T, GM_WIDTH), BF16)],
        compiler_params=pltpu.CompilerParams(dimension_semantics=("parallel",), vmem_limit_bytes=VMEM_LIMIT),
        name="inproj",
    )(h, *consts, *rope)


def _attn_tile(q_ref, k_ref, vx_ref, o_ref, t):
    l0 = t * TQ
    row_chunk = lax.broadcasted_iota(jnp.int32, (TQ, TQ), 0) // CHUNK
    col_chunk = lax.broadcasted_iota(jnp.int32, (TQ, TQ), 1) // CHUNK
    diag_mask = col_chunk <= row_chunk
    low = lax.broadcasted_iota(jnp.int32, (1, LANES), 1) < V_HEAD_DIM
    nt = (((1,), (1,)), ((), ()))

    def aligned(x):
        return x if isinstance(x, int) else pl.multiple_of(x, LANES)

    def head_pair(hp, carry):
        ocol = aligned(hp * LANES)
        vcol = aligned(hp * (2 * LANES))
        v_diag = vx_ref[l0:l0 + TQ, pl.ds(vcol, 2 * LANES)]
        outs = []
        for hh in range(2):
            col = aligned(hp * (2 * HEAD_PAD) + hh * HEAD_PAD)
            q = q_ref[:, pl.ds(col, HEAD_PAD)]
            s_d = lax.dot_general(q, k_ref[l0:l0 + TQ, pl.ds(col, HEAD_PAD)], nt, preferred_element_type=F32)
            s_d = jnp.where(diag_mask, s_d, MASKED)
            m = jnp.max(s_d, axis=-1, keepdims=True)
            if t > 0:
                s_m = lax.dot_general(q, k_ref[0:l0, pl.ds(col, HEAD_PAD)], nt, preferred_element_type=F32)
                m = jnp.maximum(m, jnp.max(s_m, axis=-1, keepdims=True))
                o = _dot(jnp.exp2((s_m - m).astype(BF16)), vx_ref[0:l0, pl.ds(vcol, 2 * LANES)])
            o_d = _dot(jnp.exp2((s_d - m).astype(BF16)), v_diag)
            o = o + o_d if t > 0 else o_d
            outs.append(o[:, :LANES] * (1.0 / o[:, LANES:]))
        o_ref[:, pl.ds(ocol, LANES)] = jnp.where(low, outs[0], outs[1])
        return carry

    if t < ATTN_UNROLL_TILES:
        for hp in range(MLA_HEADS // 2):
            head_pair(hp, 0)
    else:
        lax.fori_loop(0, MLA_HEADS // 2, head_pair, 0)


def _attn_kernel(q_ref, k_ref, v_ref, o_ref, vx_ref):
    qi = pl.program_id(1)

    @pl.when(qi == 0)
    def _():
        ones = jnp.ones((v_ref.shape[0], LANES), BF16)
        for hp in range(MLA_HEADS // 2):
            vx_ref[:, hp * 2 * LANES:hp * 2 * LANES + LANES] = v_ref[:, hp * LANES:(hp + 1) * LANES]
            vx_ref[:, hp * 2 * LANES + LANES:(hp + 1) * 2 * LANES] = ones

    for t in range(k_ref.shape[0] // TQ):
        pl.when(qi == t)(functools.partial(_attn_tile, q_ref, k_ref, vx_ref, o_ref, t))


def _attention(q, k, v, batch, seq):
    q3 = q.reshape(batch, seq, QK_WIDTH)
    k3 = k.reshape(batch, seq, QK_WIDTH)
    v3 = v.reshape(batch, seq, MLA_WIDTH)
    out = pl.pallas_call(
        _attn_kernel,
        grid=(batch, seq // TQ),
        in_specs=[pl.BlockSpec((None, TQ, QK_WIDTH), lambda b, i: (b, i, 0)),
                  pl.BlockSpec((None, seq, QK_WIDTH), lambda b, i: (b, 0, 0)),
                  pl.BlockSpec((None, seq, MLA_WIDTH), lambda b, i: (b, 0, 0))],
        out_specs=pl.BlockSpec((None, TQ, MLA_WIDTH), lambda b, i: (b, i, 0)),
        out_shape=jax.ShapeDtypeStruct((batch, seq, MLA_WIDTH), F32),
        scratch_shapes=[pltpu.VMEM((seq, MLA_HEADS * LANES), BF16)],
        compiler_params=pltpu.CompilerParams(dimension_semantics=("arbitrary", "arbitrary"),
                                             vmem_limit_bytes=VMEM_LIMIT),
        name="attn",
    )(q3, k3, v3)
    return out.reshape(batch * seq, MLA_WIDTH)


def _mix_kernel(h_ref, a_ref, u_ref, vn_ref, ws_ref, bs_ref, goa_ref, gog_ref, wout_ref, gffn_ref, wr_ref,
                h1_ref, xn_ref, route_ref, cnt_ref):
    t_chunk = lax.broadcasted_iota(jnp.int32, (GM_BLOCK, GM_BLOCK), 0) // CHUNK
    j_chunk = lax.broadcasted_iota(jnp.int32, (GM_BLOCK, GM_BLOCK), 1) // CHUNK
    w_mask = j_chunk <= t_chunk
    lane = lax.broadcasted_iota(jnp.int32, (1, LANES), 1)
    low = lane < GM_GROUP_DIM
    ws = [jnp.where(w_mask, ws_ref[g], 0.0).astype(BF16) for g in range(GM_GROUPS)]
    bs = bs_ref[...]
    m_blocks = []
    for nb in range(TM_MIX // GM_BLOCK):
        rows = slice(nb * GM_BLOCK, (nb + 1) * GM_BLOCK)
        cols = []
        for j in range(GM_WIDTH // LANES):
            sl = slice(j * LANES, (j + 1) * LANES)
            x = vn_ref[rows, sl]
            s = jnp.where(low, _dot(ws[2 * j], x), _dot(ws[2 * j + 1], x)) + bs[:, sl]
            cols.append(u_ref[rows, sl] * s)
        m_blocks.append(jnp.concatenate(cols, axis=-1))
    m = jnp.concatenate(m_blocks, axis=0)
    an = _rms(a_ref[...], goa_ref[...], MLA_WIDTH).astype(BF16)
    mn = _rms(m, gog_ref[...], GM_WIDTH).astype(BF16)
    wout = wout_ref[...]
    h1 = h_ref[...] + _dot(an, wout[:MLA_WIDTH]) + _dot(mn, wout[MLA_WIDTH:])
    h1_ref[...] = h1
    xn = _rms(h1, gffn_ref[...], D_MODEL).astype(BF16)
    xn_ref[...] = xn

    logits = _dot(xn, wr_ref[...])
    big = float(LANES)
    lane_f = lane.astype(F32)
    grp_f = (lane // EXPERTS_PER_GROUP).astype(F32)
    is_g = (lane >= N_EXPERTS) & (lane < N_EXPERTS + N_EXPERT_GROUPS)
    lg = jnp.where(is_g, logits, NEG)
    gmax = jnp.max(lg, axis=-1, keepdims=True)
    g_sel = jnp.min(jnp.where(is_g & (lg == gmax), lane_f - N_EXPERTS, big), axis=-1, keepdims=True)
    denom = jnp.sum(jnp.where(is_g, jnp.exp(lg - gmax), 0.0), axis=-1, keepdims=True)
    p_sel = 1.0 / denom
    in_grp = (lane < N_EXPERTS) & (grp_f == g_sel)
    le = jnp.where(in_grp, logits, NEG)
    t1 = jnp.max(le, axis=-1, keepdims=True)
    i1 = jnp.min(jnp.where(in_grp & (le == t1), lane_f, big), axis=-1, keepdims=True)
    in2 = in_grp & (lane_f != i1)
    le2 = jnp.where(in2, logits, NEG)
    t2 = jnp.max(le2, axis=-1, keepdims=True)
    i2 = jnp.min(jnp.where(in2 & (le2 == t2), lane_f, big), axis=-1, keepdims=True)
    e2 = jnp.exp(t2 - t1)
    tot = 1.0 + e2
    g1 = p_sel * (1.0 / tot)
    g2 = p_sel * (e2 / tot)
    route = jnp.where(lane == 0, i1,
                      jnp.where(lane == 1, i2,
                                jnp.where(lane == 2, g1, jnp.where(lane == 3, g2, 0.0))))
    route_ref[...] = route
    hits = jnp.where((lane_f == i1) | (lane_f == i2), 1.0, 0.0)
    cnt_ref[...] = jnp.broadcast_to(jnp.sum(hits, axis=0, keepdims=True), (SUBLANES, LANES))


def _mix(h, a, u, vn, lw):
    T = h.shape[0]
    nt = T // TM_MIX
    tok = lambda w: pl.BlockSpec((TM_MIX, w), lambda i: (i, 0))
    full = lambda x: pl.BlockSpec(x.shape, lambda i: (0,) * x.ndim)
    consts = [lw['w_s'], lw['b_s'], lw['g_oa'], lw['g_og'], lw['w_out'], lw['g_ffn'], lw['w_r']]
    return pl.pallas_call(
        _mix_kernel,
        grid=(nt,),
        in_specs=[tok(D_MODEL), tok(MLA_WIDTH), tok(GM_WIDTH), tok(GM_WIDTH)] + [full(x) for x in consts],
        out_specs=[tok(D_MODEL), tok(D_MODEL), tok(LANES), pl.BlockSpec((None, SUBLANES, LANES), lambda i: (i, 0, 0))],
        out_shape=[jax.ShapeDtypeStruct((T, D_MODEL), F32), jax.ShapeDtypeStruct((T, D_MODEL), BF16),
                   jax.ShapeDtypeStruct((T, LANES), F32), jax.ShapeDtypeStruct((nt, SUBLANES, LANES), F32)],
        compiler_params=pltpu.CompilerParams(dimension_semantics=("parallel",), vmem_limit_bytes=VMEM_LIMIT),
        name="mix",
    )(h, a, u, vn, *consts)


def _chunk_copy(src_ref, src_row, dst_ref, dst_row, sem):
    return pltpu.make_async_copy(src_ref.at[pl.ds(pl.multiple_of(src_row, SUBLANES), SUBLANES), :],
                                 dst_ref.at[pl.ds(pl.multiple_of(dst_row, SUBLANES), SUBLANES), :], sem)


def _dispatch_kernel(cdst_ref, nch_ref, zstart_ref, nz_ref, nused_ref, route_ref, xn_ref, lstart_ref,
                     xs_hbm, pos_ref, sorted_buf, zero_buf, sems, zsem):
    i = pl.program_id(0)
    nt = pl.num_programs(0)
    slot = i % 2
    route = route_ref[...]
    lane = lax.broadcasted_iota(jnp.int32, (1, LANES), 1)
    lane_f = lane.astype(F32)
    oh0 = lane_f == route[:, 0:1]
    oh1 = lane_f == route[:, 1:2]
    earlier = lax.broadcasted_iota(jnp.int32, (TM_MOE, TM_MOE), 1) < lax.broadcasted_iota(jnp.int32, (TM_MOE, TM_MOE), 0)
    tri = jnp.where(earlier, 1.0, 0.0).astype(BF16)
    oh0_f = jnp.where(oh0, 1.0, 0.0)
    c0 = _dot(tri, oh0_f.astype(BF16))
    c1 = _dot(tri, jnp.where(oh1, 1.0, 0.0).astype(BF16))
    tot0 = jnp.sum(oh0_f, axis=0, keepdims=True)
    ls = lstart_ref[0:1, :]
    pos0 = jnp.sum(jnp.where(oh0, ls + c0, 0.0), axis=-1, keepdims=True)
    pos1 = jnp.sum(jnp.where(oh1, ls + tot0 + c1, 0.0), axis=-1, keepdims=True)
    pos = jnp.where(lane == 0, pos0, jnp.where(lane == 1, pos1, 0.0))
    pos_ref[...] = pos
    pos_t = pos.T
    r0 = pos_t[0:1, :].astype(jnp.int32)
    r1 = pos_t[1:2, :].astype(jnp.int32)
    rid = lax.broadcasted_iota(jnp.int32, (SORTED_ROWS, TM_MOE), 0)
    perm = jnp.where((rid == r0) | (rid == r1), 1.0, 0.0).astype(BF16)
    sorted_buf[slot] = _dot(perm, xn_ref[...])

    def issue(c, carry):
        _chunk_copy(sorted_buf.at[slot], c * SUBLANES, xs_hbm, cdst_ref[i * N_CHUNKS + c], sems.at[slot]).start()
        return carry

    lax.fori_loop(0, nch_ref[i], issue, 0)

    @pl.when(i == 0)
    def _():
        zero_buf[...] = jnp.zeros_like(zero_buf)

        def per_expert(e, carry):
            def zissue(j, c2):
                _chunk_copy(zero_buf, 0, xs_hbm, zstart_ref[e] + j * SUBLANES, zsem).start()
                return c2

            def zwait(j, c2):
                _chunk_copy(zero_buf, 0, xs_hbm, 0, zsem).wait()
                return c2

            lax.fori_loop(0, nz_ref[e], zissue, 0)
            lax.fori_loop(0, nz_ref[e], zwait, 0)
            return carry

        lax.fori_loop(0, N_EXPERTS, per_expert, 0)

        def tail_copy(b):
            return pltpu.make_async_copy(zero_buf, xs_hbm.at[pl.ds(pl.multiple_of(b * BM, BM), BM), :], zsem)

        def tail_issue(b, carry):
            tail_copy(b).start()
            return carry

        def tail_wait(b, carry):
            tail_copy(b).wait()
            return carry

        n_blocks = xs_hbm.shape[0] // BM
        lax.fori_loop(nused_ref[0], n_blocks, tail_issue, 0)
        lax.fori_loop(nused_ref[0], n_blocks, tail_wait, 0)

    def drain(n, sl):
        def wait(c, carry):
            _chunk_copy(sorted_buf.at[sl], 0, xs_hbm, 0, sems.at[sl]).wait()
            return carry

        lax.fori_loop(0, n, wait, 0)

    @pl.when(i > 0)
    def _():
        drain(nch_ref[i - 1], 1 - slot)

    @pl.when(i == nt - 1)
    def _():
        drain(nch_ref[i], slot)


def _dispatch(route, xn, plan, n_slots):
    T = route.shape[0]
    nt = T // TM_MOE
    grid_spec = pltpu.PrefetchScalarGridSpec(
        num_scalar_prefetch=5,
        grid=(nt,),
        in_specs=[pl.BlockSpec((TM_MOE, LANES), lambda i, *_: (i, 0)),
                  pl.BlockSpec((TM_MOE, D_MODEL), lambda i, *_: (i, 0)),
                  pl.BlockSpec((None, SUBLANES, LANES), lambda i, *_: (i, 0, 0))],
        out_specs=[pl.BlockSpec(memory_space=pl.ANY),
                   pl.BlockSpec((TM_MOE, LANES), lambda i, *_: (i, 0))],
        scratch_shapes=[pltpu.VMEM((2, SORTED_ROWS, D_MODEL), F32), pltpu.VMEM((BM, D_MODEL), F32),
                        pltpu.SemaphoreType.DMA((2,)), pltpu.SemaphoreType.DMA(())],
    )
    return pl.pallas_call(
        _dispatch_kernel,
        grid_spec=grid_spec,
        out_shape=[jax.ShapeDtypeStruct((n_slots, D_MODEL), F32), jax.ShapeDtypeStruct((T, LANES), F32)],
        compiler_params=pltpu.CompilerParams(dimension_semantics=("arbitrary",), vmem_limit_bytes=VMEM_LIMIT),
        name="dispatch",
    )(plan['cdst'], plan['nch'], plan['zstart'], plan['nz'], plan['n_used'], route, xn, plan['lstart'])


def _expert_kernel(be_ref, nused_ref, x_ref, w1_ref, w3_ref, w2_ref, y_ref, w1_b, w3_b, w2_b):
    i = pl.program_id(0)

    @pl.when((i == 0) | (be_ref[i] != be_ref[jnp.maximum(i - 1, 0)]))
    def _():
        w1_b[...] = w1_ref[...].astype(BF16)
        w3_b[...] = w3_ref[...].astype(BF16)
        w2_b[...] = w2_ref[...].astype(BF16)

    @pl.when(i < nused_ref[0])
    def _():
        x = x_ref[...].astype(BF16)
        a = _dot(x, w1_b[...])
        b = _dot(x, w3_b[...])
        act = (jax.nn.silu(a) * b).astype(BF16)
        y_ref[...] = _dot(act, w2_b[...])

    @pl.when(i >= nused_ref[0])
    def _():
        y_ref[...] = jnp.zeros_like(y_ref)


def _experts(x_slots, plan, w1, w3, w2, layer):
    n_slots = x_slots.shape[0]
    grid_spec = pltpu.PrefetchScalarGridSpec(
        num_scalar_prefetch=2,
        grid=(n_slots // BM,),
        in_specs=[pl.BlockSpec((BM, D_MODEL), lambda i, be, nu: (jnp.minimum(i, nu[0] - 1), 0)),
                  pl.BlockSpec((None, None, D_MODEL, D_EXPERT), lambda i, be, nu: (layer, be[i], 0, 0)),
                  pl.BlockSpec((None, None, D_MODEL, D_EXPERT), lambda i, be, nu: (layer, be[i], 0, 0)),
                  pl.BlockSpec((None, None, D_EXPERT, D_MODEL), lambda i, be, nu: (layer, be[i], 0, 0))],
        out_specs=pl.BlockSpec((BM, D_MODEL), lambda i, be, nu: (i, 0)),
        scratch_shapes=[pltpu.VMEM((D_MODEL, D_EXPERT), BF16), pltpu.VMEM((D_MODEL, D_EXPERT), BF16),
                        pltpu.VMEM((D_EXPERT, D_MODEL), BF16)],
    )
    return pl.pallas_call(
        _expert_kernel,
        grid_spec=grid_spec,
        out_shape=jax.ShapeDtypeStruct((n_slots, D_MODEL), F32),
        compiler_params=pltpu.CompilerParams(dimension_semantics=("arbitrary",), vmem_limit_bytes=VMEM_LIMIT),
        name="experts",
    )(plan['block_expert'], plan['n_used'], x_slots, w1, w3, w2)


def _ple_kernel(cdst_ref, nch_ref, h_ref, pos_ref, route_ref, p_ref, gple_ref, wg_ref, bg_ref, wp_ref, y_hbm,
                o_ref, y_buf, sems):
    i = pl.program_id(0)
    nt = pl.num_programs(0)
    slot = i % 2

    def gather(tile, sl):
        def issue(c, carry):
            _chunk_copy(y_hbm, cdst_ref[tile * N_CHUNKS + c], y_buf.at[sl], c * SUBLANES, sems.at[sl]).start()
            return carry

        lax.fori_loop(0, nch_ref[tile], issue, 0)

    @pl.when(i == 0)
    def _():
        y_buf[...] = jnp.zeros_like(y_buf)
        gather(0, 0)

    @pl.when(i + 1 < nt)
    def _():
        gather(i + 1, 1 - slot)

    def wait(c, carry):
        _chunk_copy(y_hbm, 0, y_buf.at[slot], 0, sems.at[slot]).wait()
        return carry

    lax.fori_loop(0, nch_ref[i], wait, 0)

    route = route_ref[...]
    pos = pos_ref[...]
    row = lax.broadcasted_iota(jnp.int32, (1, SORTED_ROWS), 1).astype(F32)
    unsort = (jnp.where(row == pos[:, 0:1], route[:, 2:3], 0.0)
              + jnp.where(row == pos[:, 1:2], route[:, 3:4], 0.0)).astype(BF16)
    h2 = h_ref[...] + _dot(unsort, y_buf[slot].astype(BF16))
    hn = _rms(h2, gple_ref[...], D_MODEL).astype(BF16)
    gate = jax.nn.sigmoid(_dot(hn, wg_ref[...]) + bg_ref[...])
    pw = _dot(p_ref[...].astype(BF16), wp_ref[...])
    o_ref[...] = h2 + pw * gate


def _ple(h1, y_slots, pos, route, p, plan, lw):
    T = h1.shape[0]
    tok = lambda w: pl.BlockSpec((TM_MOE, w), lambda i, *_: (i, 0))
    full = lambda x: pl.BlockSpec(x.shape, lambda i, *_: (0,) * x.ndim)
    consts = [lw['g_ple'], lw['w_gate'], lw['b_gate'], lw['w_ple']]
    grid_spec = pltpu.PrefetchScalarGridSpec(
        num_scalar_prefetch=2,
        grid=(T // TM_MOE,),
        in_specs=[tok(D_MODEL), tok(LANES), tok(LANES), tok(PLE_DIM)] + [full(x) for x in consts]
                 + [pl.BlockSpec(memory_space=pl.ANY)],
        out_specs=tok(D_MODEL),
        scratch_shapes=[pltpu.VMEM((2, SORTED_ROWS, D_MODEL), F32), pltpu.SemaphoreType.DMA((2,))],
    )
    return pl.pallas_call(
        _ple_kernel,
        grid_spec=grid_spec,
        out_shape=jax.ShapeDtypeStruct((T, D_MODEL), F32),
        compiler_params=pltpu.CompilerParams(dimension_semantics=("arbitrary",), vmem_limit_bytes=VMEM_LIMIT),
        name="ple",
    )(plan['cdst'], plan['nch'], h1, pos, route, p, *consts, y_slots)


def _head_lane_src():
    src = [QK_DIM] * HEAD_PAD
    for j in range(ROPE_HALF):
        src[j] = QK_NOPE_DIM + j
        src[HEAD_PAD // 2 + j] = QK_NOPE_DIM + ROPE_HALF + j
    nope_a = HEAD_PAD // 2 - ROPE_HALF
    for j in range(nope_a):
        src[ROPE_HALF + j] = j
    for j in range(QK_NOPE_DIM - nope_a):
        src[HEAD_PAD // 2 + ROPE_HALF + j] = nope_a + j
    return jnp.asarray(src, dtype=jnp.int32)


def _lay_heads(w, heads):
    lead = w.shape[:-1]
    w = w.reshape(lead + (heads, QK_DIM))
    w = jnp.pad(w, [(0, 0)] * (len(lead) + 1) + [(0, 1)])
    w = jnp.take(w, _head_lane_src(), axis=-1)
    return w.reshape(lead + (heads * HEAD_PAD,))


def _layer_weights(i, w):
    s0 = Q_RANK
    s1 = s0 + KV_RANK
    s2 = s1 + QK_ROPE_DIM
    s3 = s2 + GM_WIDTH
    w_in = w['w_in'][i]
    kr_cols = _lay_heads(jnp.pad(w_in[:, s1:s2], ((0, 0), (QK_NOPE_DIM, 0))), 1)
    w_in_l = jnp.concatenate([w_in[:, :s1], kr_cols, w_in[:, s2:s3], w_in[:, s3:]], axis=1)
    w_ukv = w['w_ukv'][i].reshape(KV_RANK, MLA_HEADS, QK_NOPE_DIM + V_HEAD_DIM)
    w_uk = jnp.pad(w_ukv[:, :, :QK_NOPE_DIM], ((0, 0), (0, 0), (0, QK_ROPE_DIM)))
    w_uk = _lay_heads(w_uk.reshape(KV_RANK, MLA_HEADS * QK_DIM), MLA_HEADS)
    w_uv = w_ukv[:, :, QK_NOPE_DIM:].reshape(KV_RANK, MLA_WIDTH)
    w_r = jnp.concatenate([w['w_router_expert'][i], w['w_router_group'][i]], axis=1)
    w_r = jnp.pad(w_r, ((0, 0), (0, LANES - w_r.shape[1])))
    b_s = jnp.repeat(w['b_s'][i].T, GM_GROUP_DIM, axis=1)
    return {
        'g_mix': w['g_mix_norm'][i].reshape(1, -1),
        'w_in': w_in_l.astype(BF16),
        'g_cq': w['g_cq'][i].reshape(1, -1),
        'g_ckv': w['g_ckv'][i].reshape(1, -1),
        'w_uq': _lay_heads(w['w_uq'][i], MLA_HEADS).astype(BF16),
        'w_uk': w_uk.astype(BF16),
        'w_uv': w_uv.astype(BF16),
        'g_q': _lay_heads(w['g_qn'][i] * (QK_DIM ** -0.5 * LOG2_E), 1).reshape(1, HEAD_PAD),
        'g_k': _lay_heads(w['g_kn'][i], 1).reshape(1, HEAD_PAD),
        'g_v': w['g_v'][i].reshape(1, GM_WIDTH),
        'w_s': w['w_s'][i],
        'b_s': b_s,
        'g_oa': w['g_out_mla'][i].reshape(1, -1),
        'g_og': w['g_out_gmlp'][i].reshape(1, -1),
        'w_out': w['w_out'][i].astype(BF16),
        'g_ffn': w['g_ffn_norm'][i].reshape(1, -1),
        'w_r': w_r.astype(BF16),
        'g_ple': w['g_ple'][i].reshape(1, -1),
        'w_gate': w['w_ple_gate'][i].astype(BF16),
        'b_gate': w['b_ple_gate'][i].reshape(1, -1),
        'w_ple': w['w_ple'][i].astype(BF16),
    }


def _rope_tables(seq):
    inv = ROPE_THETA ** (-jnp.arange(0, QK_ROPE_DIM, 2, dtype=F32) / QK_ROPE_DIM)
    ang = jnp.arange(seq, dtype=F32)[:, None] * inv[None, :]
    cos, sin = jnp.cos(ang), jnp.sin(ang)
    rc = jnp.concatenate([jnp.ones((seq, QK_NOPE_DIM), F32), cos, cos], axis=1)
    rs = jnp.concatenate([jnp.zeros((seq, QK_NOPE_DIM), F32), -sin, sin], axis=1)
    return _lay_heads(rc, 1), _lay_heads(rs, 1)


def _slot_rows(T):
    worst = TOP_K * T + (T // TM_MOE) * N_EXPERTS * (SUBLANES - 1) + N_EXPERTS * (BM - SUBLANES)
    return -(-worst // BM) * BM


def _moe_plan(cnt_mix, T):
    nt = T // TM_MOE
    i32 = jnp.int32
    cnt = cnt_mix[:, 0, :N_EXPERTS].astype(i32).reshape(nt, TM_MOE // TM_MIX, N_EXPERTS).sum(axis=1)
    run = (cnt + SUBLANES - 1) // SUBLANES * SUBLANES
    lend = jnp.cumsum(run, axis=1)
    lstart = lend - run
    rows_e = jnp.sum(run, axis=0)
    padded = (rows_e + BM - 1) // BM * BM
    p_end = jnp.cumsum(padded)
    p_start = p_end - padded
    gbase = p_start[None, :] + jnp.cumsum(run, axis=0) - run
    c8 = jnp.arange(N_CHUNKS, dtype=i32)[None, :, None] * SUBLANES
    in_run = (c8 >= lstart[:, None, :]) & (c8 < lend[:, None, :])
    cdst = jnp.sum(jnp.where(in_run, gbase[:, None, :] + c8 - lstart[:, None, :], 0), axis=2)
    n_blocks = _slot_rows(T) // BM
    block_start = jnp.arange(n_blocks, dtype=i32) * BM
    block_expert = jnp.minimum(jnp.sum(block_start[:, None] >= p_end[None, :], axis=1), N_EXPERTS - 1)
    lstart_f = jnp.pad(lstart.astype(F32), ((0, 0), (0, LANES - N_EXPERTS)))
    return {
        'cdst': cdst.reshape(-1).astype(i32),
        'nch': (lend[:, -1] // SUBLANES).astype(i32),
        'zstart': (p_start + rows_e).astype(i32),
        'nz': ((padded - rows_e) // SUBLANES).astype(i32),
        'lstart': jnp.broadcast_to(lstart_f[:, None, :], (nt, SUBLANES, LANES)),
        'block_expert': block_expert.astype(i32),
        'n_used': (p_end[-1] // BM).astype(i32).reshape(1),
    }


def kernel(x, p, g_mix_norm, w_in, g_cq, g_ckv, w_uq, w_ukv, g_qn, g_kn, g_v, w_s, b_s, g_out_mla, g_out_gmlp,
           w_out, g_ffn_norm, w_router_group, w_router_expert, w1, w3, w2, g_ple, w_ple_gate, b_ple_gate, w_ple):
    w = dict(g_mix_norm=g_mix_norm, w_in=w_in, g_cq=g_cq, g_ckv=g_ckv, w_uq=w_uq, w_ukv=w_ukv, g_qn=g_qn,
             g_kn=g_kn, g_v=g_v, w_s=w_s, b_s=b_s, g_out_mla=g_out_mla, g_out_gmlp=g_out_gmlp, w_out=w_out,
             g_ffn_norm=g_ffn_norm, w_router_group=w_router_group, w_router_expert=w_router_expert,
             w1=w1, w3=w3, w2=w2, g_ple=g_ple, w_ple_gate=w_ple_gate, b_ple_gate=b_ple_gate, w_ple=w_ple)
    batch, seq, d = x.shape
    depth = p.shape[0]
    T = batch * seq
    rope = _rope_tables(seq)
    h = x.reshape(T, d)
    for i in range(depth):
        lw = _layer_weights(i, w)
        q, k, v, u, vn = _inproj(h, lw, rope, seq)
        a = _attention(q, k, v, batch, seq)
        h1, xn, route, cnt = _mix(h, a, u, vn, lw)
        plan = _moe_plan(cnt, T)
        x_slots, pos = _dispatch(route, xn, plan, _slot_rows(T))
        y_slots = _experts(x_slots, plan, w1, w3, w2, i)
        h = _ple(h1, y_slots, pos, route, p[i].reshape(T, PLE_DIM), plan, lw)
    return h.reshape(batch, seq, d)
```

```python
import functools

import jax
import jax.numpy as jnp
from jax import lax
from jax.experimental import pallas as pl
from jax.experimental.pallas import tpu as pltpu

D_MODEL = 1024
CHUNK = 64
MLA_HEADS = 8
QK_NOPE_DIM = 64
QK_ROPE_DIM = 32
QK_DIM = QK_NOPE_DIM + QK_ROPE_DIM
V_HEAD_DIM = 64
Q_RANK = 256
KV_RANK = 128
ROPE_THETA = 10000.0
MLA_WIDTH = MLA_HEADS * V_HEAD_DIM
GM_GROUPS = 8
GM_GROUP_DIM = 64
GM_WIDTH = GM_GROUPS * GM_GROUP_DIM
GM_BLOCK = 128
N_EXPERT_GROUPS = 4
EXPERTS_PER_GROUP = 8
N_EXPERTS = N_EXPERT_GROUPS * EXPERTS_PER_GROUP
TOP_K = 2
D_EXPERT = 512
PLE_DIM = 256
RMS_EPS = 1e-6

LANES = 128
HEAD_PAD = LANES
QK_WIDTH = MLA_HEADS * HEAD_PAD
ROPE_HALF = QK_ROPE_DIM // 2
COL_CQ = 0
COL_CKV = COL_CQ + Q_RANK
COL_KR = COL_CKV + KV_RANK
COL_U = COL_KR + HEAD_PAD
COL_V = COL_U + GM_WIDTH
IN_COLS = COL_V + GM_WIDTH

SUBLANES = 8
MXU_TILE = 256

TM_IN = 512
SUB_IN = 256
TQ = 256
ATTN_UNROLL_TILES = 8
TM_MIX = 512
BM = 512
TM_MOE = 512
SORTED_ROWS = -(-(TOP_K * TM_MOE + N_EXPERTS * (SUBLANES - 1)) // LANES) * LANES
N_CHUNKS = SORTED_ROWS // SUBLANES
VMEM_LIMIT = 48 * 1024 * 1024

F32 = jnp.float32
BF16 = jnp.bfloat16
NEG = float(jnp.finfo(jnp.float32).min)
MASKED = -1e30
LOG2_E = 1.4426950408889634


def _dot(a, b):
    return jnp.dot(a, b, preferred_element_type=F32)


def _rms(x, g, width):
    ss = jnp.sum(x * x, axis=-1, keepdims=True) * (1.0 / width)
    return (x * lax.rsqrt(ss + RMS_EPS)) * g


def _inproj_kernel(h_ref, gmix_ref, win_ref, gcq_ref, gckv_ref, wuq_ref, wuk_ref, wuv_ref,
                   gq_ref, gk_ref, gv_ref, rc_ref, rs_ref,
                   q_ref, k_ref, v_ref, u_ref, vn_ref):
    gq, gk, gv = gq_ref[...], gk_ref[...], gv_ref[...]
    ri = lax.broadcasted_iota(jnp.int32, (MXU_TILE, MXU_TILE), 0)
    ci = lax.broadcasted_iota(jnp.int32, (MXU_TILE, MXU_TILE), 1)
    head_ones = jnp.where(ri // HEAD_PAD == ci // HEAD_PAD, 1.0, 0.0).astype(BF16)
    group_ones = jnp.where(ri // GM_GROUP_DIM == ci // GM_GROUP_DIM, 1.0, 0.0).astype(BF16)

    def block_sums(x, ones_blocks):
        sq = (x * x).astype(BF16)
        return jnp.concatenate([_dot(sq[:, c:c + MXU_TILE], ones_blocks) for c in range(0, x.shape[1], MXU_TILE)],
                               axis=-1)

    staged = []
    for r in range(TM_IN // SUB_IN):
        rows = slice(r * SUB_IN, (r + 1) * SUB_IN)
        hn = _rms(h_ref[rows, :], gmix_ref[...], D_MODEL).astype(BF16)
        proj = _dot(hn, win_ref[...])
        cq = proj[:, COL_CQ:COL_CQ + Q_RANK]
        ckv = proj[:, COL_CKV:COL_CKV + KV_RANK]
        cqn = _rms(cq, gcq_ref[...], Q_RANK).astype(BF16)
        ckvn = _rms(ckv, gckv_ref[...], KV_RANK).astype(BF16)
        q_raw = _dot(cqn, wuq_ref[...])
        k_nope = _dot(ckvn, wuk_ref[...])
        v_ref[rows, :] = _dot(ckvn, wuv_ref[...]).astype(BF16)
        staged.append((rows, proj, q_raw, k_nope))

    for rows, proj, q_raw, k_nope in staged:
        kr = proj[:, COL_KR:COL_KR + HEAD_PAD]
        rc, rs = rc_ref[rows, :], rs_ref[rows, :]

        def rope(x):
            return x * rc + pltpu.roll(x, HEAD_PAD // 2, 1) * rs

        rk = rope(kr * gk)
        ss_kr = jnp.sum(kr * kr, axis=-1, keepdims=True)
        heads = [slice(hd * HEAD_PAD, (hd + 1) * HEAD_PAD) for hd in range(MLA_HEADS)]
        gvv = jax.nn.gelu(proj[:, COL_V:COL_V + GM_WIDTH])
        ss_q = block_sums(q_raw, head_ones)
        ss_k = block_sums(k_nope, head_ones) + ss_kr
        ss_v = block_sums(gvv, group_ones)
        inv_q = lax.rsqrt(ss_q * (1.0 / QK_DIM) + RMS_EPS)
        inv_k = lax.rsqrt(ss_k * (1.0 / QK_DIM) + RMS_EPS)
        inv_v = lax.rsqrt(ss_v * (1.0 / GM_GROUP_DIM) + RMS_EPS)
        for sl in heads:
            q_ref[rows, sl] = rope((q_raw[:, sl] * inv_q[:, sl]) * gq).astype(BF16)
            k_ref[rows, sl] = ((k_nope[:, sl] * gk + rk) * inv_k[:, sl]).astype(BF16)
        vn_ref[rows, :] = ((gvv * inv_v) * gv).astype(BF16)
        u_ref[rows, :] = jax.nn.gelu(proj[:, COL_U:COL_U + GM_WIDTH])


def _inproj(h, lw, rope, seq):
    T = h.shape[0]
    nt = T // TM_IN
    per_seq = seq // TM_IN
    tok = lambda w: pl.BlockSpec((TM_IN, w), lambda i: (i, 0))
    full = lambda a: pl.BlockSpec(a.shape, lambda i: (0,) * a.ndim)
    pos = pl.BlockSpec((TM_IN, LANES), lambda i: (i % per_seq, 0))
    consts = [lw['g_mix'], lw['w_in'], lw['g_cq'], lw['g_ckv'], lw['w_uq'], lw['w_uk'], lw['w_uv'],
              lw['g_q'], lw['g_k'], lw['g_v']]
    return pl.pallas_call(
        _inproj_kernel,
        grid=(nt,),
        in_specs=[tok(D_MODEL)] + [full(a) for a in consts] + [pos, pos],
        out_specs=[tok(QK_WIDTH), tok(QK_WIDTH), tok(MLA_WIDTH), tok(GM_WIDTH), tok(GM_WIDTH)],
        out_shape=[jax.ShapeDtypeStruct((T, QK_WIDTH), BF16), jax.ShapeDtypeStruct((T, QK_WIDTH), BF16),
                   jax.ShapeDtypeStruct((T, MLA_WIDTH), BF16), jax.ShapeDtypeStruct((T, GM_WIDTH), F32),
                   jax.ShapeDtypeStruct((T, GM_WIDTH), BF16)],
        compiler_params=pltpu.CompilerParams(dimension_semantics=("parallel",), vmem_limit_bytes=VMEM_LIMIT),
        name="inproj",
    )(h, *consts, *rope)


def _attn_tile(q_ref, k_ref, vx_ref, o_ref, t):
    l0 = t * TQ
    row_chunk = lax.broadcasted_iota(jnp.int32, (TQ, TQ), 0) // CHUNK
    col_chunk = lax.broadcasted_iota(jnp.int32, (TQ, TQ), 1) // CHUNK
    diag_mask = col_chunk <= row_chunk
    low = lax.broadcasted_iota(jnp.int32, (1, LANES), 1) < V_HEAD_DIM
    nt = (((1,), (1,)), ((), ()))

    def aligned(x):
        return x if isinstance(x, int) else pl.multiple_of(x, LANES)

    def head_pair(hp, carry):
        ocol = aligned(hp * LANES)
        vcol = aligned(hp * (2 * LANES))
        v_diag = vx_ref[l0:l0 + TQ, pl.ds(vcol, 2 * LANES)]
        outs = []
        for hh in range(2):
            col = aligned(hp * (2 * HEAD_PAD) + hh * HEAD_PAD)
            q = q_ref[:, pl.ds(col, HEAD_PAD)]
            s_d = lax.dot_general(q, k_ref[l0:l0 + TQ, pl.ds(col, HEAD_PAD)], nt, preferred_element_type=F32)
            s_d = jnp.where(diag_mask, s_d, MASKED)
            m = jnp.max(s_d, axis=-1, keepdims=True)
            if t > 0:
                s_m = lax.dot_general(q, k_ref[0:l0, pl.ds(col, HEAD_PAD)], nt, preferred_element_type=F32)
                m = jnp.maximum(m, jnp.max(s_m, axis=-1, keepdims=True))
                o = _dot(jnp.exp2((s_m - m).astype(BF16)), vx_ref[0:l0, pl.ds(vcol, 2 * LANES)])
            o_d = _dot(jnp.exp2((s_d - m).astype(BF16)), v_diag)
            o = o + o_d if t > 0 else o_d
            outs.append(o[:, :LANES] * (1.0 / o[:, LANES:]))
        o_ref[:, pl.ds(ocol, LANES)] = jnp.where(low, outs[0], outs[1])
        return carry

    if t < ATTN_UNROLL_TILES:
        for hp in range(MLA_HEADS // 2):
            head_pair(hp, 0)
    else:
        lax.fori_loop(0, MLA_HEADS // 2, head_pair, 0)


def _attn_kernel(q_ref, k_ref, v_ref, o_ref, vx_ref):
    qi = pl.program_id(1)

    @pl.when(qi == 0)
    def _():
        ones = jnp.ones((v_ref.shape[0], LANES), BF16)
        for hp in range(MLA_HEADS // 2):
            vx_ref[:, hp * 2 * LANES:hp * 2 * LANES + LANES] = v_ref[:, hp * LANES:(hp + 1) * LANES]
            vx_ref[:, hp * 2 * LANES + LANES:(hp + 1) * 2 * LANES] = ones

    for t in range(k_ref.shape[0] // TQ):
        pl.when(qi == t)(functools.partial(_attn_tile, q_ref, k_ref, vx_ref, o_ref, t))


def _attention(q, k, v, batch, seq):
    q3 = q.reshape(batch, seq, QK_WIDTH)
    k3 = k.reshape(batch, seq, QK_WIDTH)
    v3 = v.reshape(batch, seq, MLA_WIDTH)
    out = pl.pallas_call(
        _attn_kernel,
        grid=(batch, seq // TQ),
        in_specs=[pl.BlockSpec((None, TQ, QK_WIDTH), lambda b, i: (b, i, 0)),
                  pl.BlockSpec((None, seq, QK_WIDTH), lambda b, i: (b, 0, 0)),
                  pl.BlockSpec((None, seq, MLA_WIDTH), lambda b, i: (b, 0, 0))],
        out_specs=pl.BlockSpec((None, TQ, MLA_WIDTH), lambda b, i: (b, i, 0)),
        out_shape=jax.ShapeDtypeStruct((batch, seq, MLA_WIDTH), F32),
        scratch_shapes=[pltpu.VMEM((seq, MLA_HEADS * LANES), BF16)],
        compiler_params=pltpu.CompilerParams(dimension_semantics=("arbitrary", "arbitrary"),
                                             vmem_limit_bytes=VMEM_LIMIT),
        name="attn",
    )(q3, k3, v3)
    return out.reshape(batch * seq, MLA_WIDTH)


def _mix_kernel(h_ref, a_ref, u_ref, vn_ref, ws_ref, bs_ref, goa_ref, gog_ref, wout_ref, gffn_ref, wr_ref,
                h1_ref, xn_ref, route_ref, cnt_ref):
    t_chunk = lax.broadcasted_iota(jnp.int32, (GM_BLOCK, GM_BLOCK), 0) // CHUNK
    j_chunk = lax.broadcasted_iota(jnp.int32, (GM_BLOCK, GM_BLOCK), 1) // CHUNK
    w_mask = j_chunk <= t_chunk
    lane = lax.broadcasted_iota(jnp.int32, (1, LANES), 1)
    low = lane < GM_GROUP_DIM
    ws = [jnp.where(w_mask, ws_ref[g], 0.0).astype(BF16) for g in range(GM_GROUPS)]
    bs = bs_ref[...]
    m_blocks = []
    for nb in range(TM_MIX // GM_BLOCK):
        rows = slice(nb * GM_BLOCK, (nb + 1) * GM_BLOCK)
        cols = []
        for j in range(GM_WIDTH // LANES):
            sl = slice(j * LANES, (j + 1) * LANES)
            x = vn_ref[rows, sl]
            s = jnp.where(low, _dot(ws[2 * j], x), _dot(ws[2 * j + 1], x)) + bs[:, sl]
            cols.append(u_ref[rows, sl] * s)
        m_blocks.append(jnp.concatenate(cols, axis=-1))
    m = jnp.concatenate(m_blocks, axis=0)
    an = _rms(a_ref[...], goa_ref[...], MLA_WIDTH).astype(BF16)
    mn = _rms(m, gog_ref[...], GM_WIDTH).astype(BF16)
    wout = wout_ref[...]
    h1 = h_ref[...] + _dot(an, wout[:MLA_WIDTH]) + _dot(mn, wout[MLA_WIDTH:])
    h1_ref[...] = h1
    xn = _rms(h1, gffn_ref[...], D_MODEL).astype(BF16)
    xn_ref[...] = xn

    logits = _dot(xn, wr_ref[...])
    big = float(LANES)
    lane_f = lane.astype(F32)
    grp_f = (lane // EXPERTS_PER_GROUP).astype(F32)
    is_g = (lane >= N_EXPERTS) & (lane < N_EXPERTS + N_EXPERT_GROUPS)
    lg = jnp.where(is_g, logits, NEG)
    gmax = jnp.max(lg, axis=-1, keepdims=True)
    g_sel = jnp.min(jnp.where(is_g & (lg == gmax), lane_f - N_EXPERTS, big), axis=-1, keepdims=True)
    denom = jnp.sum(jnp.where(is_g, jnp.exp(lg - gmax), 0.0), axis=-1, keepdims=True)
    p_sel = 1.0 / denom
    in_grp = (lane < N_EXPERTS) & (grp_f == g_sel)
    le = jnp.where(in_grp, logits, NEG)
    t1 = jnp.max(le, axis=-1, keepdims=True)
    i1 = jnp.min(jnp.where(in_grp & (le == t1), lane_f, big), axis=-1, keepdims=True)
    in2 = in_grp & (lane_f != i1)
    le2 = jnp.where(in2, logits, NEG)
    t2 = jnp.max(le2, axis=-1, keepdims=True)
    i2 = jnp.min(jnp.where(in2 & (le2 == t2), lane_f, big), axis=-1, keepdims=True)
    e2 = jnp.exp(t2 - t1)
    tot = 1.0 + e2
    g1 = p_sel * (1.0 / tot)
    g2 = p_sel * (e2 / tot)
    route = jnp.where(lane == 0, i1,
                      jnp.where(lane == 1, i2,
                                jnp.where(lane == 2, g1, jnp.where(lane == 3, g2, 0.0))))
    route_ref[...] = route
    hits = jnp.where((lane_f == i1) | (lane_f == i2), 1.0, 0.0)
    cnt_ref[...] = jnp.broadcast_to(jnp.sum(hits, axis=0, keepdims=True), (SUBLANES, LANES))


def _mix(h, a, u, vn, lw):
    T = h.shape[0]
    nt = T // TM_MIX
    tok = lambda w: pl.BlockSpec((TM_MIX, w), lambda i: (i, 0))
    full = lambda x: pl.BlockSpec(x.shape, lambda i: (0,) * x.ndim)
    consts = [lw['w_s'], lw['b_s'], lw['g_oa'], lw['g_og'], lw['w_out'], lw['g_ffn'], lw['w_r']]
    return pl.pallas_call(
        _mix_kernel,
        grid=(nt,),
        in_specs=[tok(D_MODEL), tok(MLA_WIDTH), tok(GM_WIDTH), tok(GM_WIDTH)] + [full(x) for x in consts],
        out_specs=[tok(D_MODEL), tok(D_MODEL), tok(LANES), pl.BlockSpec((None, SUBLANES, LANES), lambda i: (i, 0, 0))],
        out_shape=[jax.ShapeDtypeStruct((T, D_MODEL), F32), jax.ShapeDtypeStruct((T, D_MODEL), BF16),
                   jax.ShapeDtypeStruct((T, LANES), F32), jax.ShapeDtypeStruct((nt, SUBLANES, LANES), F32)],
        compiler_params=pltpu.CompilerParams(dimension_semantics=("parallel",), vmem_limit_bytes=VMEM_LIMIT),
        name="mix",
    )(h, a, u, vn, *consts)


def _chunk_copy(src_ref, src_row, dst_ref, dst_row, sem):
    return pltpu.make_async_copy(src_ref.at[pl.ds(pl.multiple_of(src_row, SUBLANES), SUBLANES), :],
                                 dst_ref.at[pl.ds(pl.multiple_of(dst_row, SUBLANES), SUBLANES), :], sem)


def _dispatch_kernel(cdst_ref, nch_ref, zstart_ref, nz_ref, nused_ref, route_ref, xn_ref, lstart_ref,
                     xs_hbm, pos_ref, sorted_buf, zero_buf, sems, zsem):
    i = pl.program_id(0)
    nt = pl.num_programs(0)
    slot = i % 2
    route = route_ref[...]
    lane = lax.broadcasted_iota(jnp.int32, (1, LANES), 1)
    lane_f = lane.astype(F32)
    oh0 = lane_f == route[:, 0:1]
    oh1 = lane_f == route[:, 1:2]
    earlier = lax.broadcasted_iota(jnp.int32, (TM_MOE, TM_MOE), 1) < lax.broadcasted_iota(jnp.int32, (TM_MOE, TM_MOE), 0)
    tri = jnp.where(earlier, 1.0, 0.0).astype(BF16)
    oh0_f = jnp.where(oh0, 1.0, 0.0)
    c0 = _dot(tri, oh0_f.astype(BF16))
    c1 = _dot(tri, jnp.where(oh1, 1.0, 0.0).astype(BF16))
    tot0 = jnp.sum(oh0_f, axis=0, keepdims=True)
    ls = lstart_ref[0:1, :]
    pos0 = jnp.sum(jnp.where(oh0, ls + c0, 0.0), axis=-1, keepdims=True)
    pos1 = jnp.sum(jnp.where(oh1, ls + tot0 + c1, 0.0), axis=-1, keepdims=True)
    pos = jnp.where(lane == 0, pos0, jnp.where(lane == 1, pos1, 0.0))
    pos_ref[...] = pos
    pos_t = pos.T
    r0 = pos_t[0:1, :].astype(jnp.int32)
    r1 = pos_t[1:2, :].astype(jnp.int32)
    rid = lax.broadcasted_iota(jnp.int32, (SORTED_ROWS, TM_MOE), 0)
    perm = jnp.where((rid == r0) | (rid == r1), 1.0, 0.0).astype(BF16)
    sorted_buf[slot] = _dot(perm, xn_ref[...])

    def issue(c, carry):
        _chunk_copy(sorted_buf.at[slot], c * SUBLANES, xs_hbm, cdst_ref[i * N_CHUNKS + c], sems.at[slot]).start()
        return carry

    lax.fori_loop(0, nch_ref[i], issue, 0)

    @pl.when(i == 0)
    def _():
        zero_buf[...] = jnp.zeros_like(zero_buf)

        def per_expert(e, carry):
            def zissue(j, c2):
                _chunk_copy(zero_buf, 0, xs_hbm, zstart_ref[e] + j * SUBLANES, zsem).start()
                return c2

            def zwait(j, c2):
                _chunk_copy(zero_buf, 0, xs_hbm, 0, zsem).wait()
                return c2

            lax.fori_loop(0, nz_ref[e], zissue, 0)
            lax.fori_loop(0, nz_ref[e], zwait, 0)
            return carry

        lax.fori_loop(0, N_EXPERTS, per_expert, 0)

        def tail_copy(b):
            return pltpu.make_async_copy(zero_buf, xs_hbm.at[pl.ds(pl.multiple_of(b * BM, BM), BM), :], zsem)

        def tail_issue(b, carry):
            tail_copy(b).start()
            return carry

        def tail_wait(b, carry):
            tail_copy(b).wait()
            return carry

        n_blocks = xs_hbm.shape[0] // BM
        lax.fori_loop(nused_ref[0], n_blocks, tail_issue, 0)
        lax.fori_loop(nused_ref[0], n_blocks, tail_wait, 0)

    def drain(n, sl):
        def wait(c, carry):
            _chunk_copy(sorted_buf.at[sl], 0, xs_hbm, 0, sems.at[sl]).wait()
            return carry

        lax.fori_loop(0, n, wait, 0)

    @pl.when(i > 0)
    def _():
        drain(nch_ref[i - 1], 1 - slot)

    @pl.when(i == nt - 1)
    def _():
        drain(nch_ref[i], slot)


def _dispatch(route, xn, plan, n_slots):
    T = route.shape[0]
    nt = T // TM_MOE
    grid_spec = pltpu.PrefetchScalarGridSpec(
        num_scalar_prefetch=5,
        grid=(nt,),
        in_specs=[pl.BlockSpec((TM_MOE, LANES), lambda i, *_: (i, 0)),
                  pl.BlockSpec((TM_MOE, D_MODEL), lambda i, *_: (i, 0)),
                  pl.BlockSpec((None, SUBLANES, LANES), lambda i, *_: (i, 0, 0))],
        out_specs=[pl.BlockSpec(memory_space=pl.ANY),
                   pl.BlockSpec((TM_MOE, LANES), lambda i, *_: (i, 0))],
        scratch_shapes=[pltpu.VMEM((2, SORTED_ROWS, D_MODEL), F32), pltpu.VMEM((BM, D_MODEL), F32),
                        pltpu.SemaphoreType.DMA((2,)), pltpu.SemaphoreType.DMA(())],
    )
    return pl.pallas_call(
        _dispatch_kernel,
        grid_spec=grid_spec,
        out_shape=[jax.ShapeDtypeStruct((n_slots, D_MODEL), F32), jax.ShapeDtypeStruct((T, LANES), F32)],
        compiler_params=pltpu.CompilerParams(dimension_semantics=("arbitrary",), vmem_limit_bytes=VMEM_LIMIT),
        name="dispatch",
    )(plan['cdst'], plan['nch'], plan['zstart'], plan['nz'], plan['n_used'], route, xn, plan['lstart'])


def _expert_kernel(be_ref, nused_ref, x_ref, w1_ref, w3_ref, w2_ref, y_ref, w1_b, w3_b, w2_b):
    i = pl.program_id(0)

    @pl.when((i == 0) | (be_ref[i] != be_ref[jnp.maximum(i - 1, 0)]))
    def _():
        w1_b[...] = w1_ref[...].astype(BF16)
        w3_b[...] = w3_ref[...].astype(BF16)
        w2_b[...] = w2_ref[...].astype(BF16)

    @pl.when(i < nused_ref[0])
    def _():
        halves = [slice(r * (BM // 2), (r + 1) * (BM // 2)) for r in range(2)]
        xs = [x_ref[rows, :].astype(BF16) for rows in halves]
        ab = [(_dot(x, w1_b[...]), _dot(x, w3_b[...])) for x in xs]
        acts = [(jax.nn.silu(a) * b).astype(BF16) for a, b in ab]
        for rows, act in zip(halves, acts):
            y_ref[rows, :] = _dot(act, w2_b[...])

    @pl.when(i >= nused_ref[0])
    def _():
        y_ref[...] = jnp.zeros_like(y_ref)


def _experts(x_slots, plan, w1, w3, w2, layer):
    n_slots = x_slots.shape[0]
    grid_spec = pltpu.PrefetchScalarGridSpec(
        num_scalar_prefetch=2,
        grid=(n_slots // BM,),
        in_specs=[pl.BlockSpec((BM, D_MODEL), lambda i, be, nu: (jnp.minimum(i, nu[0] - 1), 0)),
                  pl.BlockSpec((None, None, D_MODEL, D_EXPERT), lambda i, be, nu: (layer, be[i], 0, 0)),
                  pl.BlockSpec((None, None, D_MODEL, D_EXPERT), lambda i, be, nu: (layer, be[i], 0, 0)),
                  pl.BlockSpec((None, None, D_EXPERT, D_MODEL), lambda i, be, nu: (layer, be[i], 0, 0))],
        out_specs=pl.BlockSpec((BM, D_MODEL), lambda i, be, nu: (i, 0)),
        scratch_shapes=[pltpu.VMEM((D_MODEL, D_EXPERT), BF16), pltpu.VMEM((D_MODEL, D_EXPERT), BF16),
                        pltpu.VMEM((D_EXPERT, D_MODEL), BF16)],
    )
    return pl.pallas_call(
        _expert_kernel,
        grid_spec=grid_spec,
        out_shape=jax.ShapeDtypeStruct((n_slots, D_MODEL), F32),
        compiler_params=pltpu.CompilerParams(dimension_semantics=("arbitrary",), vmem_limit_bytes=VMEM_LIMIT),
        name="experts",
    )(plan['block_expert'], plan['n_used'], x_slots, w1, w3, w2)


def _ple_kernel(cdst_ref, nch_ref, h_ref, pos_ref, route_ref, p_ref, gple_ref, wg_ref, bg_ref, wp_ref, y_hbm,
                o_ref, y_buf, sems):
    i = pl.program_id(0)
    nt = pl.num_programs(0)
    slot = i % 2

    def gather(tile, sl):
        def issue(c, carry):
            _chunk_copy(y_hbm, cdst_ref[tile * N_CHUNKS + c], y_buf.at[sl], c * SUBLANES, sems.at[sl]).start()
            return carry

        lax.fori_loop(0, nch_ref[tile], issue, 0)

    @pl.when(i == 0)
    def _():
        y_buf[...] = jnp.zeros_like(y_buf)
        gather(0, 0)

    @pl.when(i + 1 < nt)
    def _():
        gather(i + 1, 1 - slot)

    def wait(c, carry):
        _chunk_copy(y_hbm, 0, y_buf.at[slot], 0, sems.at[slot]).wait()
        return carry

    lax.fori_loop(0, nch_ref[i], wait, 0)

    route = route_ref[...]
    pos = pos_ref[...]
    row = lax.broadcasted_iota(jnp.int32, (1, SORTED_ROWS), 1).astype(F32)
    unsort = (jnp.where(row == pos[:, 0:1], route[:, 2:3], 0.0)
              + jnp.where(row == pos[:, 1:2], route[:, 3:4], 0.0)).astype(BF16)
    h2 = h_ref[...] + _dot(unsort, y_buf[slot].astype(BF16))
    hn = _rms(h2, gple_ref[...], D_MODEL).astype(BF16)
    gate = jax.nn.sigmoid(_dot(hn, wg_ref[...]) + bg_ref[...])
    pw = _dot(p_ref[...].astype(BF16), wp_ref[...])
    o_ref[...] = h2 + pw * gate


def _ple(h1, y_slots, pos, route, p, plan, lw):
    T = h1.shape[0]
    tok = lambda w: pl.BlockSpec((TM_MOE, w), lambda i, *_: (i, 0))
    full = lambda x: pl.BlockSpec(x.shape, lambda i, *_: (0,) * x.ndim)
    consts = [lw['g_ple'], lw['w_gate'], lw['b_gate'], lw['w_ple']]
    grid_spec = pltpu.PrefetchScalarGridSpec(
        num_scalar_prefetch=2,
        grid=(T // TM_MOE,),
        in_specs=[tok(D_MODEL), tok(LANES), tok(LANES), tok(PLE_DIM)] + [full(x) for x in consts]
                 + [pl.BlockSpec(memory_space=pl.ANY)],
        out_specs=tok(D_MODEL),
        scratch_shapes=[pltpu.VMEM((2, SORTED_ROWS, D_MODEL), F32), pltpu.SemaphoreType.DMA((2,))],
    )
    return pl.pallas_call(
        _ple_kernel,
        grid_spec=grid_spec,
        out_shape=jax.ShapeDtypeStruct((T, D_MODEL), F32),
        compiler_params=pltpu.CompilerParams(dimension_semantics=("arbitrary",), vmem_limit_bytes=VMEM_LIMIT),
        name="ple",
    )(plan['cdst'], plan['nch'], h1, pos, route, p, *consts, y_slots)


def _head_lane_src():
    src = [QK_DIM] * HEAD_PAD
    for j in range(ROPE_HALF):
        src[j] = QK_NOPE_DIM + j
        src[HEAD_PAD // 2 + j] = QK_NOPE_DIM + ROPE_HALF + j
    nope_a = HEAD_PAD // 2 - ROPE_HALF
    for j in range(nope_a):
        src[ROPE_HALF + j] = j
    for j in range(QK_NOPE_DIM - nope_a):
        src[HEAD_PAD // 2 + ROPE_HALF + j] = nope_a + j
    return jnp.asarray(src, dtype=jnp.int32)


def _lay_heads(w, heads):
    lead = w.shape[:-1]
    w = w.reshape(lead + (heads, QK_DIM))
    w = jnp.pad(w, [(0, 0)] * (len(lead) + 1) + [(0, 1)])
    w = jnp.take(w, _head_lane_src(), axis=-1)
    return w.reshape(lead + (heads * HEAD_PAD,))


def _layer_weights(i, w):
    s0 = Q_RANK
    s1 = s0 + KV_RANK
    s2 = s1 + QK_ROPE_DIM
    s3 = s2 + GM_WIDTH
    w_in = w['w_in'][i]
    kr_cols = _lay_heads(jnp.pad(w_in[:, s1:s2], ((0, 0), (QK_NOPE_DIM, 0))), 1)
    w_in_l = jnp.concatenate([w_in[:, :s1], kr_cols, w_in[:, s2:s3], w_in[:, s3:]], axis=1)
    w_ukv = w['w_ukv'][i].reshape(KV_RANK, MLA_HEADS, QK_NOPE_DIM + V_HEAD_DIM)
    w_uk = jnp.pad(w_ukv[:, :, :QK_NOPE_DIM], ((0, 0), (0, 0), (0, QK_ROPE_DIM)))
    w_uk = _lay_heads(w_uk.reshape(KV_RANK, MLA_HEADS * QK_DIM), MLA_HEADS)
    w_uv = w_ukv[:, :, QK_NOPE_DIM:].reshape(KV_RANK, MLA_WIDTH)
    w_r = jnp.concatenate([w['w_router_expert'][i], w['w_router_group'][i]], axis=1)
    w_r = jnp.pad(w_r, ((0, 0), (0, LANES - w_r.shape[1])))
    b_s = jnp.repeat(w['b_s'][i].T, GM_GROUP_DIM, axis=1)
    return {
        'g_mix': w['g_mix_norm'][i].reshape(1, -1),
        'w_in': w_in_l.astype(BF16),
        'g_cq': w['g_cq'][i].reshape(1, -1),
        'g_ckv': w['g_ckv'][i].reshape(1, -1),
        'w_uq': _lay_heads(w['w_uq'][i], MLA_HEADS).astype(BF16),
        'w_uk': w_uk.astype(BF16),
        'w_uv': w_uv.astype(BF16),
        'g_q': _lay_heads(w['g_qn'][i] * (QK_DIM ** -0.5 * LOG2_E), 1).reshape(1, HEAD_PAD),
        'g_k': _lay_heads(w['g_kn'][i], 1).reshape(1, HEAD_PAD),
        'g_v': w['g_v'][i].reshape(1, GM_WIDTH),
        'w_s': w['w_s'][i],
        'b_s': b_s,
        'g_oa': w['g_out_mla'][i].reshape(1, -1),
        'g_og': w['g_out_gmlp'][i].reshape(1, -1),
        'w_out': w['w_out'][i].astype(BF16),
        'g_ffn': w['g_ffn_norm'][i].reshape(1, -1),
        'w_r': w_r.astype(BF16),
        'g_ple': w['g_ple'][i].reshape(1, -1),
        'w_gate': w['w_ple_gate'][i].astype(BF16),
        'b_gate': w['b_ple_gate'][i].reshape(1, -1),
        'w_ple': w['w_ple'][i].astype(BF16),
    }


def _rope_tables(seq):
    inv = ROPE_THETA ** (-jnp.arange(0, QK_ROPE_DIM, 2, dtype=F32) / QK_ROPE_DIM)
    ang = jnp.arange(seq, dtype=F32)[:, None] * inv[None, :]
    cos, sin = jnp.cos(ang), jnp.sin(ang)
    rc = jnp.concatenate([jnp.ones((seq, QK_NOPE_DIM), F32), cos, cos], axis=1)
    rs = jnp.concatenate([jnp.zeros((seq, QK_NOPE_DIM), F32), -sin, sin], axis=1)
    return _lay_heads(rc, 1), _lay_heads(rs, 1)


def _slot_rows(T):
    worst = TOP_K * T + (T // TM_MOE) * N_EXPERTS * (SUBLANES - 1) + N_EXPERTS * (BM - SUBLANES)
    return -(-worst // BM) * BM


def _moe_plan(cnt_mix, T):
    nt = T // TM_MOE
    i32 = jnp.int32
    cnt = cnt_mix[:, 0, :N_EXPERTS].astype(i32).reshape(nt, TM_MOE // TM_MIX, N_EXPERTS).sum(axis=1)
    run = (cnt + SUBLANES - 1) // SUBLANES * SUBLANES
    lend = jnp.cumsum(run, axis=1)
    lstart = lend - run
    rows_e = jnp.sum(run, axis=0)
    padded = (rows_e + BM - 1) // BM * BM
    p_end = jnp.cumsum(padded)
    p_start = p_end - padded
    gbase = p_start[None, :] + jnp.cumsum(run, axis=0) - run
    c8 = jnp.arange(N_CHUNKS, dtype=i32)[None, :, None] * SUBLANES
    in_run = (c8 >= lstart[:, None, :]) & (c8 < lend[:, None, :])
    cdst = jnp.sum(jnp.where(in_run, gbase[:, None, :] + c8 - lstart[:, None, :], 0), axis=2)
    n_blocks = _slot_rows(T) // BM
    block_start = jnp.arange(n_blocks, dtype=i32) * BM
    block_expert = jnp.minimum(jnp.sum(block_start[:, None] >= p_end[None, :], axis=1), N_EXPERTS - 1)
    lstart_f = jnp.pad(lstart.astype(F32), ((0, 0), (0, LANES - N_EXPERTS)))
    return {
        'cdst': cdst.reshape(-1).astype(i32),
        'nch': (lend[:, -1] // SUBLANES).astype(i32),
        'zstart': (p_start + rows_e).astype(i32),
        'nz': ((padded - rows_e) // SUBLANES).astype(i32),
        'lstart': jnp.broadcast_to(lstart_f[:, None, :], (nt, SUBLANES, LANES)),
        'block_expert': block_expert.astype(i32),
        'n_used': (p_end[-1] // BM).astype(i32).reshape(1),
    }


def kernel(x, p, g_mix_norm, w_in, g_cq, g_ckv, w_uq, w_ukv, g_qn, g_kn, g_v, w_s, b_s, g_out_mla, g_out_gmlp,
           w_out, g_ffn_norm, w_router_group, w_router_expert, w1, w3, w2, g_ple, w_ple_gate, b_ple_gate, w_ple):
    w = dict(g_mix_norm=g_mix_norm, w_in=w_in, g_cq=g_cq, g_ckv=g_ckv, w_uq=w_uq, w_ukv=w_ukv, g_qn=g_qn,
             g_kn=g_kn, g_v=g_v, w_s=w_s, b_s=b_s, g_out_mla=g_out_mla, g_out_gmlp=g_out_gmlp, w_out=w_out,
             g_ffn_norm=g_ffn_norm, w_router_group=w_router_group, w_router_expert=w_router_expert,
             w1=w1, w3=w3, w2=w2, g_ple=g_ple, w_ple_gate=w_ple_gate, b_ple_gate=b_ple_gate, w_ple=w_ple)
    batch, seq, d = x.shape
    depth = p.shape[0]
    T = batch * seq
    rope = _rope_tables(seq)
    h = x.reshape(T, d)
    for i in range(depth):
        lw = _layer_weights(i, w)
        q, k, v, u, vn = _inproj(h, lw, rope, seq)
        a = _attention(q, k, v, batch, seq)
        h1, xn, route, cnt = _mix(h, a, u, vn, lw)
        plan = _moe_plan(cnt, T)
        x_slots, pos = _dispatch(route, xn, plan, _slot_rows(T))
        y_slots = _experts(x_slots, plan, w1, w3, w2, i)
        h = _ple(h1, y_slots, pos, route, p[i].reshape(T, PLE_DIM), plan, lw)
    return h.reshape(batch, seq, d)
```

```python
import functools

import jax
import jax.numpy as jnp
from jax import lax
from jax.experimental import pallas as pl
from jax.experimental.pallas import tpu as pltpu

D_MODEL = 1024
CHUNK = 64
MLA_HEADS = 8
QK_NOPE_DIM = 64
QK_ROPE_DIM = 32
QK_DIM = QK_NOPE_DIM + QK_ROPE_DIM
V_HEAD_DIM = 64
Q_RANK = 256
KV_RANK = 128
ROPE_THETA = 10000.0
MLA_WIDTH = MLA_HEADS * V_HEAD_DIM
GM_GROUPS = 8
GM_GROUP_DIM = 64
GM_WIDTH = GM_GROUPS * GM_GROUP_DIM
GM_BLOCK = 128
N_EXPERT_GROUPS = 4
EXPERTS_PER_GROUP = 8
N_EXPERTS = N_EXPERT_GROUPS * EXPERTS_PER_GROUP
TOP_K = 2
D_EXPERT = 512
PLE_DIM = 256
RMS_EPS = 1e-6

LANES = 128
HEAD_PAD = LANES
QK_WIDTH = MLA_HEADS * HEAD_PAD
ROPE_HALF = QK_ROPE_DIM // 2
COL_CQ = 0
COL_CKV = COL_CQ + Q_RANK
COL_KR = COL_CKV + KV_RANK
COL_U = COL_KR + HEAD_PAD
COL_V = COL_U + GM_WIDTH
IN_COLS = COL_V + GM_WIDTH

SUBLANES = 8
MXU_TILE = 256

TM_IN = 512
SUB_IN = 256
TQ = 256
ATTN_UNROLL_TILES = 8
TM_MIX = 512
BM = 512
X_RING = 3
TM_MOE = 512
SORTED_ROWS = -(-(TOP_K * TM_MOE + N_EXPERTS * (SUBLANES - 1)) // LANES) * LANES
N_CHUNKS = SORTED_ROWS // SUBLANES
VMEM_LIMIT = 48 * 1024 * 1024

F32 = jnp.float32
BF16 = jnp.bfloat16
NEG = float(jnp.finfo(jnp.float32).min)
MASKED = -1e30
LOG2_E = 1.4426950408889634


def _dot(a, b):
    return jnp.dot(a, b, preferred_element_type=F32)


def _rms(x, g, width):
    ss = jnp.sum(x * x, axis=-1, keepdims=True) * (1.0 / width)
    return (x * lax.rsqrt(ss + RMS_EPS)) * g


def _inproj_kernel(h_ref, gmix_ref, win_ref, gcq_ref, gckv_ref, wuq_ref, wuk_ref, wuv_ref,
                   gq_ref, gk_ref, gv_ref, rc_ref, rs_ref,
                   q_ref, k_ref, v_ref, u_ref, vn_ref):
    gq, gk, gv = gq_ref[...], gk_ref[...], gv_ref[...]
    ri = lax.broadcasted_iota(jnp.int32, (MXU_TILE, MXU_TILE), 0)
    ci = lax.broadcasted_iota(jnp.int32, (MXU_TILE, MXU_TILE), 1)
    head_ones = jnp.where(ri // HEAD_PAD == ci // HEAD_PAD, 1.0, 0.0).astype(BF16)
    group_ones = jnp.where(ri // GM_GROUP_DIM == ci // GM_GROUP_DIM, 1.0, 0.0).astype(BF16)

    def block_sums(x, ones_blocks):
        sq = (x * x).astype(BF16)
        return jnp.concatenate([_dot(sq[:, c:c + MXU_TILE], ones_blocks) for c in range(0, x.shape[1], MXU_TILE)],
                               axis=-1)

    staged = []
    for r in range(TM_IN // SUB_IN):
        rows = slice(r * SUB_IN, (r + 1) * SUB_IN)
        hn = _rms(h_ref[rows, :], gmix_ref[...], D_MODEL).astype(BF16)
        proj = _dot(hn, win_ref[...])
        cq = proj[:, COL_CQ:COL_CQ + Q_RANK]
        ckv = proj[:, COL_CKV:COL_CKV + KV_RANK]
        cqn = _rms(cq, gcq_ref[...], Q_RANK).astype(BF16)
        ckvn = _rms(ckv, gckv_ref[...], KV_RANK).astype(BF16)
        q_raw = _dot(cqn, wuq_ref[...])
        k_nope = _dot(ckvn, wuk_ref[...])
        v_ref[rows, :] = _dot(ckvn, wuv_ref[...]).astype(BF16)
        staged.append((rows, proj, q_raw, k_nope))

    for rows, proj, q_raw, k_nope in staged:
        kr = proj[:, COL_KR:COL_KR + HEAD_PAD]
        rc, rs = rc_ref[rows, :], rs_ref[rows, :]

        def rope(x):
            return x * rc + pltpu.roll(x, HEAD_PAD // 2, 1) * rs

        rk = rope(kr * gk)
        ss_kr = jnp.sum(kr * kr, axis=-1, keepdims=True)
        heads = [slice(hd * HEAD_PAD, (hd + 1) * HEAD_PAD) for hd in range(MLA_HEADS)]
        gvv = jax.nn.gelu(proj[:, COL_V:COL_V + GM_WIDTH])
        ss_q = block_sums(q_raw, head_ones)
        ss_k = block_sums(k_nope, head_ones) + ss_kr
        ss_v = block_sums(gvv, group_ones)
        inv_q = lax.rsqrt(ss_q * (1.0 / QK_DIM) + RMS_EPS)
        inv_k = lax.rsqrt(ss_k * (1.0 / QK_DIM) + RMS_EPS)
        inv_v = lax.rsqrt(ss_v * (1.0 / GM_GROUP_DIM) + RMS_EPS)
        for sl in heads:
            q_ref[rows, sl] = rope((q_raw[:, sl] * inv_q[:, sl]) * gq).astype(BF16)
            k_ref[rows, sl] = ((k_nope[:, sl] * gk + rk) * inv_k[:, sl]).astype(BF16)
        vn_ref[rows, :] = ((gvv * inv_v) * gv).astype(BF16)
        u_ref[rows, :] = jax.nn.gelu(proj[:, COL_U:COL_U + GM_WIDTH])


def _inproj(h, lw, rope, seq):
    T = h.shape[0]
    nt = T // TM_IN
    per_seq = seq // TM_IN
    tok = lambda w: pl.BlockSpec((TM_IN, w), lambda i: (i, 0))
    full = lambda a: pl.BlockSpec(a.shape, lambda i: (0,) * a.ndim)
    pos = pl.BlockSpec((TM_IN, LANES), lambda i: (i % per_seq, 0))
    consts = [lw['g_mix'], lw['w_in'], lw['g_cq'], lw['g_ckv'], lw['w_uq'], lw['w_uk'], lw['w_uv'],
              lw['g_q'], lw['g_k'], lw['g_v']]
    return pl.pallas_call(
        _inproj_kernel,
        grid=(nt,),
        in_specs=[tok(D_MODEL)] + [full(a) for a in consts] + [pos, pos],
        out_specs=[tok(QK_WIDTH), tok(QK_WIDTH), tok(MLA_WIDTH), tok(GM_WIDTH), tok(GM_WIDTH)],
        out_shape=[jax.ShapeDtypeStruct((T, QK_WIDTH), BF16), jax.ShapeDtypeStruct((T, QK_WIDTH), BF16),
                   jax.ShapeDtypeStruct((T, MLA_WIDTH), BF16), jax.ShapeDtypeStruct((T, GM_WIDTH), F32),
                   jax.ShapeDtypeStruct((T, GM_WIDTH), BF16)],
        compiler_params=pltpu.CompilerParams(dimension_semantics=("parallel",), vmem_limit_bytes=VMEM_LIMIT),
        name="inproj",
    )(h, *consts, *rope)


def _attn_tile(q_ref, k_ref, vx_ref, o_ref, t):
    l0 = t * TQ
    row_chunk = lax.broadcasted_iota(jnp.int32, (TQ, TQ), 0) // CHUNK
    col_chunk = lax.broadcasted_iota(jnp.int32, (TQ, TQ), 1) // CHUNK
    diag_mask = col_chunk <= row_chunk
    low = lax.broadcasted_iota(jnp.int32, (1, LANES), 1) < V_HEAD_DIM
    nt = (((1,), (1,)), ((), ()))

    def aligned(x):
        return x if isinstance(x, int) else pl.multiple_of(x, LANES)

    def head_pair(hp, carry):
        ocol = aligned(hp * LANES)
        vcol = aligned(hp * (2 * LANES))
        v_diag = vx_ref[l0:l0 + TQ, pl.ds(vcol, 2 * LANES)]
        outs = []
        for hh in range(2):
            col = aligned(hp * (2 * HEAD_PAD) + hh * HEAD_PAD)
            q = q_ref[:, pl.ds(col, HEAD_PAD)]
            s_d = lax.dot_general(q, k_ref[l0:l0 + TQ, pl.ds(col, HEAD_PAD)], nt, preferred_element_type=F32)
            s_d = jnp.where(diag_mask, s_d, MASKED)
            m = jnp.max(s_d, axis=-1, keepdims=True)
            if t > 0:
                s_m = lax.dot_general(q, k_ref[0:l0, pl.ds(col, HEAD_PAD)], nt, preferred_element_type=F32)
                m = jnp.maximum(m, jnp.max(s_m, axis=-1, keepdims=True))
                o = _dot(jnp.exp2((s_m - m).astype(BF16)), vx_ref[0:l0, pl.ds(vcol, 2 * LANES)])
            o_d = _dot(jnp.exp2((s_d - m).astype(BF16)), v_diag)
            o = o + o_d if t > 0 else o_d
            outs.append(o[:, :LANES] * (1.0 / o[:, LANES:]))
        o_ref[:, pl.ds(ocol, LANES)] = jnp.where(low, outs[0], outs[1])
        return carry

    if t < ATTN_UNROLL_TILES:
        for hp in range(MLA_HEADS // 2):
            head_pair(hp, 0)
    else:
        lax.fori_loop(0, MLA_HEADS // 2, head_pair, 0)


def _attn_kernel(q_ref, k_ref, v_ref, o_ref, vx_ref):
    qi = pl.program_id(1)

    @pl.when(qi == 0)
    def _():
        ones = jnp.ones((v_ref.shape[0], LANES), BF16)
        for hp in range(MLA_HEADS // 2):
            vx_ref[:, hp * 2 * LANES:hp * 2 * LANES + LANES] = v_ref[:, hp * LANES:(hp + 1) * LANES]
            vx_ref[:, hp * 2 * LANES + LANES:(hp + 1) * 2 * LANES] = ones

    for t in range(k_ref.shape[0] // TQ):
        pl.when(qi == t)(functools.partial(_attn_tile, q_ref, k_ref, vx_ref, o_ref, t))


def _attention(q, k, v, batch, seq):
    q3 = q.reshape(batch, seq, QK_WIDTH)
    k3 = k.reshape(batch, seq, QK_WIDTH)
    v3 = v.reshape(batch, seq, MLA_WIDTH)
    out = pl.pallas_call(
        _attn_kernel,
        grid=(batch, seq // TQ),
        in_specs=[pl.BlockSpec((None, TQ, QK_WIDTH), lambda b, i: (b, i, 0)),
                  pl.BlockSpec((None, seq, QK_WIDTH), lambda b, i: (b, 0, 0)),
                  pl.BlockSpec((None, seq, MLA_WIDTH), lambda b, i: (b, 0, 0))],
        out_specs=pl.BlockSpec((None, TQ, MLA_WIDTH), lambda b, i: (b, i, 0)),
        out_shape=jax.ShapeDtypeStruct((batch, seq, MLA_WIDTH), F32),
        scratch_shapes=[pltpu.VMEM((seq, MLA_HEADS * LANES), BF16)],
        compiler_params=pltpu.CompilerParams(dimension_semantics=("arbitrary", "arbitrary"),
                                             vmem_limit_bytes=VMEM_LIMIT),
        name="attn",
    )(q3, k3, v3)
    return out.reshape(batch * seq, MLA_WIDTH)


def _mix_kernel(h_ref, a_ref, u_ref, vn_ref, ws_ref, bs_ref, goa_ref, gog_ref, wout_ref, gffn_ref, wr_ref,
                h1_ref, xn_ref, route_ref, cnt_ref):
    t_chunk = lax.broadcasted_iota(jnp.int32, (GM_BLOCK, GM_BLOCK), 0) // CHUNK
    j_chunk = lax.broadcasted_iota(jnp.int32, (GM_BLOCK, GM_BLOCK), 1) // CHUNK
    w_mask = j_chunk <= t_chunk
    lane = lax.broadcasted_iota(jnp.int32, (1, LANES), 1)
    low = lane < GM_GROUP_DIM
    ws = [jnp.where(w_mask, ws_ref[g], 0.0).astype(BF16) for g in range(GM_GROUPS)]
    bs = bs_ref[...]
    m_blocks = []
    for nb in range(TM_MIX // GM_BLOCK):
        rows = slice(nb * GM_BLOCK, (nb + 1) * GM_BLOCK)
        cols = []
        for j in range(GM_WIDTH // LANES):
            sl = slice(j * LANES, (j + 1) * LANES)
            x = vn_ref[rows, sl]
            s = jnp.where(low, _dot(ws[2 * j], x), _dot(ws[2 * j + 1], x)) + bs[:, sl]
            cols.append(u_ref[rows, sl] * s)
        m_blocks.append(jnp.concatenate(cols, axis=-1))
    m = jnp.concatenate(m_blocks, axis=0)
    an = _rms(a_ref[...], goa_ref[...], MLA_WIDTH).astype(BF16)
    mn = _rms(m, gog_ref[...], GM_WIDTH).astype(BF16)
    wout = wout_ref[...]
    h1 = h_ref[...] + _dot(an, wout[:MLA_WIDTH]) + _dot(mn, wout[MLA_WIDTH:])
    h1_ref[...] = h1
    xn = _rms(h1, gffn_ref[...], D_MODEL).astype(BF16)
    xn_ref[...] = xn

    logits = _dot(xn, wr_ref[...])
    big = float(LANES)
    lane_f = lane.astype(F32)
    grp_f = (lane // EXPERTS_PER_GROUP).astype(F32)
    is_g = (lane >= N_EXPERTS) & (lane < N_EXPERTS + N_EXPERT_GROUPS)
    lg = jnp.where(is_g, logits, NEG)
    gmax = jnp.max(lg, axis=-1, keepdims=True)
    g_sel = jnp.min(jnp.where(is_g & (lg == gmax), lane_f - N_EXPERTS, big), axis=-1, keepdims=True)
    denom = jnp.sum(jnp.where(is_g, jnp.exp(lg - gmax), 0.0), axis=-1, keepdims=True)
    p_sel = 1.0 / denom
    in_grp = (lane < N_EXPERTS) & (grp_f == g_sel)
    le = jnp.where(in_grp, logits, NEG)
    t1 = jnp.max(le, axis=-1, keepdims=True)
    i1 = jnp.min(jnp.where(in_grp & (le == t1), lane_f, big), axis=-1, keepdims=True)
    in2 = in_grp & (lane_f != i1)
    le2 = jnp.where(in2, logits, NEG)
    t2 = jnp.max(le2, axis=-1, keepdims=True)
    i2 = jnp.min(jnp.where(in2 & (le2 == t2), lane_f, big), axis=-1, keepdims=True)
    e2 = jnp.exp(t2 - t1)
    tot = 1.0 + e2
    g1 = p_sel * (1.0 / tot)
    g2 = p_sel * (e2 / tot)
    route = jnp.where(lane == 0, i1,
                      jnp.where(lane == 1, i2,
                                jnp.where(lane == 2, g1, jnp.where(lane == 3, g2, 0.0))))
    route_ref[...] = route
    hits = jnp.where((lane_f == i1) | (lane_f == i2), 1.0, 0.0)
    cnt_ref[...] = jnp.broadcast_to(jnp.sum(hits, axis=0, keepdims=True), (SUBLANES, LANES))


def _mix(h, a, u, vn, lw):
    T = h.shape[0]
    nt = T // TM_MIX
    tok = lambda w: pl.BlockSpec((TM_MIX, w), lambda i: (i, 0))
    full = lambda x: pl.BlockSpec(x.shape, lambda i: (0,) * x.ndim)
    consts = [lw['w_s'], lw['b_s'], lw['g_oa'], lw['g_og'], lw['w_out'], lw['g_ffn'], lw['w_r']]
    return pl.pallas_call(
        _mix_kernel,
        grid=(nt,),
        in_specs=[tok(D_MODEL), tok(MLA_WIDTH), tok(GM_WIDTH), tok(GM_WIDTH)] + [full(x) for x in consts],
        out_specs=[tok(D_MODEL), tok(D_MODEL), tok(LANES), pl.BlockSpec((None, SUBLANES, LANES), lambda i: (i, 0, 0))],
        out_shape=[jax.ShapeDtypeStruct((T, D_MODEL), F32), jax.ShapeDtypeStruct((T, D_MODEL), BF16),
                   jax.ShapeDtypeStruct((T, LANES), F32), jax.ShapeDtypeStruct((nt, SUBLANES, LANES), F32)],
        compiler_params=pltpu.CompilerParams(dimension_semantics=("parallel",), vmem_limit_bytes=VMEM_LIMIT),
        name="mix",
    )(h, a, u, vn, *consts)


def _chunk_copy(src_ref, src_row, dst_ref, dst_row, sem):
    return pltpu.make_async_copy(src_ref.at[pl.ds(pl.multiple_of(src_row, SUBLANES), SUBLANES), :],
                                 dst_ref.at[pl.ds(pl.multiple_of(dst_row, SUBLANES), SUBLANES), :], sem)


def _dispatch_kernel(cdst_ref, nch_ref, zstart_ref, nz_ref, nused_ref, route_ref, xn_ref, lstart_ref,
                     xs_hbm, pos_ref, sorted_buf, zero_buf, sems, zsem):
    i = pl.program_id(0)
    nt = pl.num_programs(0)
    slot = i % 2
    route = route_ref[...]
    lane = lax.broadcasted_iota(jnp.int32, (1, LANES), 1)
    lane_f = lane.astype(F32)
    oh0 = lane_f == route[:, 0:1]
    oh1 = lane_f == route[:, 1:2]
    earlier = lax.broadcasted_iota(jnp.int32, (TM_MOE, TM_MOE), 1) < lax.broadcasted_iota(jnp.int32, (TM_MOE, TM_MOE), 0)
    tri = jnp.where(earlier, 1.0, 0.0).astype(BF16)
    oh0_f = jnp.where(oh0, 1.0, 0.0)
    c0 = _dot(tri, oh0_f.astype(BF16))
    c1 = _dot(tri, jnp.where(oh1, 1.0, 0.0).astype(BF16))
    tot0 = jnp.sum(oh0_f, axis=0, keepdims=True)
    ls = lstart_ref[0:1, :]
    pos0 = jnp.sum(jnp.where(oh0, ls + c0, 0.0), axis=-1, keepdims=True)
    pos1 = jnp.sum(jnp.where(oh1, ls + tot0 + c1, 0.0), axis=-1, keepdims=True)
    pos = jnp.where(lane == 0, pos0, jnp.where(lane == 1, pos1, 0.0))
    pos_ref[...] = pos
    pos_t = pos.T
    r0 = pos_t[0:1, :].astype(jnp.int32)
    r1 = pos_t[1:2, :].astype(jnp.int32)
    rid = lax.broadcasted_iota(jnp.int32, (SORTED_ROWS, TM_MOE), 0)
    perm = jnp.where((rid == r0) | (rid == r1), 1.0, 0.0).astype(BF16)
    sorted_buf[slot] = _dot(perm, xn_ref[...])

    def issue(c, carry):
        _chunk_copy(sorted_buf.at[slot], c * SUBLANES, xs_hbm, cdst_ref[i * N_CHUNKS + c], sems.at[slot]).start()
        return carry

    lax.fori_loop(0, nch_ref[i], issue, 0)

    @pl.when(i == 0)
    def _():
        zero_buf[...] = jnp.zeros_like(zero_buf)

        def per_expert(e, carry):
            def zissue(j, c2):
                _chunk_copy(zero_buf, 0, xs_hbm, zstart_ref[e] + j * SUBLANES, zsem).start()
                return c2

            def zwait(j, c2):
                _chunk_copy(zero_buf, 0, xs_hbm, 0, zsem).wait()
                return c2

            lax.fori_loop(0, nz_ref[e], zissue, 0)
            lax.fori_loop(0, nz_ref[e], zwait, 0)
            return carry

        lax.fori_loop(0, N_EXPERTS, per_expert, 0)

        def tail_copy(b):
            return pltpu.make_async_copy(zero_buf, xs_hbm.at[pl.ds(pl.multiple_of(b * BM, BM), BM), :], zsem)

        def tail_issue(b, carry):
            tail_copy(b).start()
            return carry

        def tail_wait(b, carry):
            tail_copy(b).wait()
            return carry

        n_blocks = xs_hbm.shape[0] // BM
        lax.fori_loop(nused_ref[0], n_blocks, tail_issue, 0)
        lax.fori_loop(nused_ref[0], n_blocks, tail_wait, 0)

    def drain(n, sl):
        def wait(c, carry):
            _chunk_copy(sorted_buf.at[sl], 0, xs_hbm, 0, sems.at[sl]).wait()
            return carry

        lax.fori_loop(0, n, wait, 0)

    @pl.when(i > 0)
    def _():
        drain(nch_ref[i - 1], 1 - slot)

    @pl.when(i == nt - 1)
    def _():
        drain(nch_ref[i], slot)


def _dispatch(route, xn, plan, n_slots):
    T = route.shape[0]
    nt = T // TM_MOE
    grid_spec = pltpu.PrefetchScalarGridSpec(
        num_scalar_prefetch=5,
        grid=(nt,),
        in_specs=[pl.BlockSpec((TM_MOE, LANES), lambda i, *_: (i, 0)),
                  pl.BlockSpec((TM_MOE, D_MODEL), lambda i, *_: (i, 0)),
                  pl.BlockSpec((None, SUBLANES, LANES), lambda i, *_: (i, 0, 0))],
        out_specs=[pl.BlockSpec(memory_space=pl.ANY),
                   pl.BlockSpec((TM_MOE, LANES), lambda i, *_: (i, 0))],
        scratch_shapes=[pltpu.VMEM((2, SORTED_ROWS, D_MODEL), F32), pltpu.VMEM((BM, D_MODEL), F32),
                        pltpu.SemaphoreType.DMA((2,)), pltpu.SemaphoreType.DMA(())],
    )
    return pl.pallas_call(
        _dispatch_kernel,
        grid_spec=grid_spec,
        out_shape=[jax.ShapeDtypeStruct((n_slots, D_MODEL), F32), jax.ShapeDtypeStruct((T, LANES), F32)],
        compiler_params=pltpu.CompilerParams(dimension_semantics=("arbitrary",), vmem_limit_bytes=VMEM_LIMIT),
        name="dispatch",
    )(plan['cdst'], plan['nch'], plan['zstart'], plan['nz'], plan['n_used'], route, xn, plan['lstart'])


def _expert_kernel(be_ref, nused_ref, w1_ref, w3_ref, w2_ref, x_hbm, y_ref, w1_b, w3_b, w2_b, x_ring, sems):
    i = pl.program_id(0)
    n_used = nused_ref[0]
    slot = i % X_RING

    def row_copy(block):
        sl = block % X_RING
        return pltpu.make_async_copy(x_hbm.at[pl.ds(pl.multiple_of(block * BM, BM), BM), :], x_ring.at[sl],
                                     sems.at[sl])

    @pl.when(i == 0)
    def _():
        for b in range(X_RING - 1):
            pl.when(b < n_used)(lambda b=b: row_copy(b).start())

    @pl.when(i + (X_RING - 1) < n_used)
    def _():
        row_copy(i + (X_RING - 1)).start()

    @pl.when((i == 0) | (be_ref[i] != be_ref[jnp.maximum(i - 1, 0)]))
    def _():
        w1_b[...] = w1_ref[...].astype(BF16)
        w3_b[...] = w3_ref[...].astype(BF16)
        w2_b[...] = w2_ref[...].astype(BF16)

    @pl.when(i < n_used)
    def _():
        row_copy(i).wait()
        halves = [slice(r * (BM // 2), (r + 1) * (BM // 2)) for r in range(2)]
        xs = [x_ring[slot, rows, :].astype(BF16) for rows in halves]
        ab = [(_dot(x, w1_b[...]), _dot(x, w3_b[...])) for x in xs]
        acts = [(jax.nn.silu(a) * b).astype(BF16) for a, b in ab]
        for rows, act in zip(halves, acts):
            y_ref[rows, :] = _dot(act, w2_b[...])

    @pl.when(i >= n_used)
    def _():
        y_ref[...] = jnp.zeros_like(y_ref)


def _experts(x_slots, plan, w1, w3, w2, layer):
    n_slots = x_slots.shape[0]
    grid_spec = pltpu.PrefetchScalarGridSpec(
        num_scalar_prefetch=2,
        grid=(n_slots // BM,),
        in_specs=[pl.BlockSpec((None, None, D_MODEL, D_EXPERT), lambda i, be, nu: (layer, be[i], 0, 0)),
                  pl.BlockSpec((None, None, D_MODEL, D_EXPERT), lambda i, be, nu: (layer, be[i], 0, 0)),
                  pl.BlockSpec((None, None, D_EXPERT, D_MODEL), lambda i, be, nu: (layer, be[i], 0, 0)),
                  pl.BlockSpec(memory_space=pl.ANY)],
        out_specs=pl.BlockSpec((BM, D_MODEL), lambda i, be, nu: (i, 0)),
        scratch_shapes=[pltpu.VMEM((D_MODEL, D_EXPERT), BF16), pltpu.VMEM((D_MODEL, D_EXPERT), BF16),
                        pltpu.VMEM((D_EXPERT, D_MODEL), BF16),
                        pltpu.VMEM((X_RING, BM, D_MODEL), F32), pltpu.SemaphoreType.DMA((X_RING,))],
    )
    return pl.pallas_call(
        _expert_kernel,
        grid_spec=grid_spec,
        out_shape=jax.ShapeDtypeStruct((n_slots, D_MODEL), F32),
        compiler_params=pltpu.CompilerParams(dimension_semantics=("arbitrary",), vmem_limit_bytes=VMEM_LIMIT),
        name="experts",
    )(plan['block_expert'], plan['n_used'], w1, w3, w2, x_slots)


def _ple_kernel(cdst_ref, nch_ref, h_ref, pos_ref, route_ref, p_ref, gple_ref, wg_ref, bg_ref, wp_ref, y_hbm,
                o_ref, y_buf, sems):
    i = pl.program_id(0)
    nt = pl.num_programs(0)
    slot = i % 2

    def gather(tile, sl):
        def issue(c, carry):
            _chunk_copy(y_hbm, cdst_ref[tile * N_CHUNKS + c], y_buf.at[sl], c * SUBLANES, sems.at[sl]).start()
            return carry

        lax.fori_loop(0, nch_ref[tile], issue, 0)

    @pl.when(i == 0)
    def _():
        y_buf[...] = jnp.zeros_like(y_buf)
        gather(0, 0)

    @pl.when(i + 1 < nt)
    def _():
        gather(i + 1, 1 - slot)

    def wait(c, carry):
        _chunk_copy(y_hbm, 0, y_buf.at[slot], 0, sems.at[slot]).wait()
        return carry

    lax.fori_loop(0, nch_ref[i], wait, 0)

    route = route_ref[...]
    pos = pos_ref[...]
    row = lax.broadcasted_iota(jnp.int32, (1, SORTED_ROWS), 1).astype(F32)
    unsort = (jnp.where(row == pos[:, 0:1], route[:, 2:3], 0.0)
              + jnp.where(row == pos[:, 1:2], route[:, 3:4], 0.0)).astype(BF16)
    h2 = h_ref[...] + _dot(unsort, y_buf[slot].astype(BF16))
    hn = _rms(h2, gple_ref[...], D_MODEL).astype(BF16)
    gate = jax.nn.sigmoid(_dot(hn, wg_ref[...]) + bg_ref[...])
    pw = _dot(p_ref[...].astype(BF16), wp_ref[...])
    o_ref[...] = h2 + pw * gate


def _ple(h1, y_slots, pos, route, p, plan, lw):
    T = h1.shape[0]
    tok = lambda w: pl.BlockSpec((TM_MOE, w), lambda i, *_: (i, 0))
    full = lambda x: pl.BlockSpec(x.shape, lambda i, *_: (0,) * x.ndim)
    consts = [lw['g_ple'], lw['w_gate'], lw['b_gate'], lw['w_ple']]
    grid_spec = pltpu.PrefetchScalarGridSpec(
        num_scalar_prefetch=2,
        grid=(T // TM_MOE,),
        in_specs=[tok(D_MODEL), tok(LANES), tok(LANES), tok(PLE_DIM)] + [full(x) for x in consts]
                 + [pl.BlockSpec(memory_space=pl.ANY)],
        out_specs=tok(D_MODEL),
        scratch_shapes=[pltpu.VMEM((2, SORTED_ROWS, D_MODEL), F32), pltpu.SemaphoreType.DMA((2,))],
    )
    return pl.pallas_call(
        _ple_kernel,
        grid_spec=grid_spec,
        out_shape=jax.ShapeDtypeStruct((T, D_MODEL), F32),
        compiler_params=pltpu.CompilerParams(dimension_semantics=("arbitrary",), vmem_limit_bytes=VMEM_LIMIT),
        name="ple",
    )(plan['cdst'], plan['nch'], h1, pos, route, p, *consts, y_slots)


def _head_lane_src():
    src = [QK_DIM] * HEAD_PAD
    for j in range(ROPE_HALF):
        src[j] = QK_NOPE_DIM + j
        src[HEAD_PAD // 2 + j] = QK_NOPE_DIM + ROPE_HALF + j
    nope_a = HEAD_PAD // 2 - ROPE_HALF
    for j in range(nope_a):
        src[ROPE_HALF + j] = j
    for j in range(QK_NOPE_DIM - nope_a):
        src[HEAD_PAD // 2 + ROPE_HALF + j] = nope_a + j
    return jnp.asarray(src, dtype=jnp.int32)


def _lay_heads(w, heads):
    lead = w.shape[:-1]
    w = w.reshape(lead + (heads, QK_DIM))
    w = jnp.pad(w, [(0, 0)] * (len(lead) + 1) + [(0, 1)])
    w = jnp.take(w, _head_lane_src(), axis=-1)
    return w.reshape(lead + (heads * HEAD_PAD,))


def _layer_weights(i, w):
    s0 = Q_RANK
    s1 = s0 + KV_RANK
    s2 = s1 + QK_ROPE_DIM
    s3 = s2 + GM_WIDTH
    w_in = w['w_in'][i]
    kr_cols = _lay_heads(jnp.pad(w_in[:, s1:s2], ((0, 0), (QK_NOPE_DIM, 0))), 1)
    w_in_l = jnp.concatenate([w_in[:, :s1], kr_cols, w_in[:, s2:s3], w_in[:, s3:]], axis=1)
    w_ukv = w['w_ukv'][i].reshape(KV_RANK, MLA_HEADS, QK_NOPE_DIM + V_HEAD_DIM)
    w_uk = jnp.pad(w_ukv[:, :, :QK_NOPE_DIM], ((0, 0), (0, 0), (0, QK_ROPE_DIM)))
    w_uk = _lay_heads(w_uk.reshape(KV_RANK, MLA_HEADS * QK_DIM), MLA_HEADS)
    w_uv = w_ukv[:, :, QK_NOPE_DIM:].reshape(KV_RANK, MLA_WIDTH)
    w_r = jnp.concatenate([w['w_router_expert'][i], w['w_router_group'][i]], axis=1)
    w_r = jnp.pad(w_r, ((0, 0), (0, LANES - w_r.shape[1])))
    b_s = jnp.repeat(w['b_s'][i].T, GM_GROUP_DIM, axis=1)
    return {
        'g_mix': w['g_mix_norm'][i].reshape(1, -1),
        'w_in': w_in_l.astype(BF16),
        'g_cq': w['g_cq'][i].reshape(1, -1),
        'g_ckv': w['g_ckv'][i].reshape(1, -1),
        'w_uq': _lay_heads(w['w_uq'][i], MLA_HEADS).astype(BF16),
        'w_uk': w_uk.astype(BF16),
        'w_uv': w_uv.astype(BF16),
        'g_q': _lay_heads(w['g_qn'][i] * (QK_DIM ** -0.5 * LOG2_E), 1).reshape(1, HEAD_PAD),
        'g_k': _lay_heads(w['g_kn'][i], 1).reshape(1, HEAD_PAD),
        'g_v': w['g_v'][i].reshape(1, GM_WIDTH),
        'w_s': w['w_s'][i],
        'b_s': b_s,
        'g_oa': w['g_out_mla'][i].reshape(1, -1),
        'g_og': w['g_out_gmlp'][i].reshape(1, -1),
        'w_out': w['w_out'][i].astype(BF16),
        'g_ffn': w['g_ffn_norm'][i].reshape(1, -1),
        'w_r': w_r.astype(BF16),
        'g_ple': w['g_ple'][i].reshape(1, -1),
        'w_gate': w['w_ple_gate'][i].astype(BF16),
        'b_gate': w['b_ple_gate'][i].reshape(1, -1),
        'w_ple': w['w_ple'][i].astype(BF16),
    }


def _rope_tables(seq):
    inv = ROPE_THETA ** (-jnp.arange(0, QK_ROPE_DIM, 2, dtype=F32) / QK_ROPE_DIM)
    ang = jnp.arange(seq, dtype=F32)[:, None] * inv[None, :]
    cos, sin = jnp.cos(ang), jnp.sin(ang)
    rc = jnp.concatenate([jnp.ones((seq, QK_NOPE_DIM), F32), cos, cos], axis=1)
    rs = jnp.concatenate([jnp.zeros((seq, QK_NOPE_DIM), F32), -sin, sin], axis=1)
    return _lay_heads(rc, 1), _lay_heads(rs, 1)


def _slot_rows(T):
    worst = TOP_K * T + (T // TM_MOE) * N_EXPERTS * (SUBLANES - 1) + N_EXPERTS * (BM - SUBLANES)
    return -(-worst // BM) * BM


def _moe_plan(cnt_mix, T):
    nt = T // TM_MOE
    i32 = jnp.int32
    cnt = cnt_mix[:, 0, :N_EXPERTS].astype(i32).reshape(nt, TM_MOE // TM_MIX, N_EXPERTS).sum(axis=1)
    run = (cnt + SUBLANES - 1) // SUBLANES * SUBLANES
    lend = jnp.cumsum(run, axis=1)
    lstart = lend - run
    rows_e = jnp.sum(run, axis=0)
    padded = (rows_e + BM - 1) // BM * BM
    p_end = jnp.cumsum(padded)
    p_start = p_end - padded
    gbase = p_start[None, :] + jnp.cumsum(run, axis=0) - run
    c8 = jnp.arange(N_CHUNKS, dtype=i32)[None, :, None] * SUBLANES
    in_run = (c8 >= lstart[:, None, :]) & (c8 < lend[:, None, :])
    cdst = jnp.sum(jnp.where(in_run, gbase[:, None, :] + c8 - lstart[:, None, :], 0), axis=2)
    n_blocks = _slot_rows(T) // BM
    block_start = jnp.arange(n_blocks, dtype=i32) * BM
    block_expert = jnp.minimum(jnp.sum(block_start[:, None] >= p_end[None, :], axis=1), N_EXPERTS - 1)
    lstart_f = jnp.pad(lstart.astype(F32), ((0, 0), (0, LANES - N_EXPERTS)))
    return {
        'cdst': cdst.reshape(-1).astype(i32),
        'nch': (lend[:, -1] // SUBLANES).astype(i32),
        'zstart': (p_start + rows_e).astype(i32),
        'nz': ((padded - rows_e) // SUBLANES).astype(i32),
        'lstart': jnp.broadcast_to(lstart_f[:, None, :], (nt, SUBLANES, LANES)),
        'block_expert': block_expert.astype(i32),
        'n_used': (p_end[-1] // BM).astype(i32).reshape(1),
    }


def kernel(x, p, g_mix_norm, w_in, g_cq, g_ckv, w_uq, w_ukv, g_qn, g_kn, g_v, w_s, b_s, g_out_mla, g_out_gmlp,
           w_out, g_ffn_norm, w_router_group, w_router_expert, w1, w3, w2, g_ple, w_ple_gate, b_ple_gate, w_ple):
    w = dict(g_mix_norm=g_mix_norm, w_in=w_in, g_cq=g_cq, g_ckv=g_ckv, w_uq=w_uq, w_ukv=w_ukv, g_qn=g_qn,
             g_kn=g_kn, g_v=g_v, w_s=w_s, b_s=b_s, g_out_mla=g_out_mla, g_out_gmlp=g_out_gmlp, w_out=w_out,
             g_ffn_norm=g_ffn_norm, w_router_group=w_router_group, w_router_expert=w_router_expert,
             w1=w1, w3=w3, w2=w2, g_ple=g_ple, w_ple_gate=w_ple_gate, b_ple_gate=b_ple_gate, w_ple=w_ple)
    batch, seq, d = x.shape
    depth = p.shape[0]
    T = batch * seq
    rope = _rope_tables(seq)
    h = x.reshape(T, d)
    for i in range(depth):
        lw = _layer_weights(i, w)
        q, k, v, u, vn = _inproj(h, lw, rope, seq)
        a = _attention(q, k, v, batch, seq)
        h1, xn, route, cnt = _mix(h, a, u, vn, lw)
        plan = _moe_plan(cnt, T)
        x_slots, pos = _dispatch(route, xn, plan, _slot_rows(T))
        y_slots = _experts(x_slots, plan, w1, w3, w2, i)
        h = _ple(h1, y_slots, pos, route, p[i].reshape(T, PLE_DIM), plan, lw)
    return h.reshape(batch, seq, d)
```
